```python
import jax, jax.numpy as jnp
from jax import lax
import numpy as np

D_MODEL = 1024
BATCH = 32
SEQ = 2048
DEPTH = 1

D_MIX = D_MODEL
SB_HEADS = 8
SB_HEAD_DIM = 64
SB_WIDTH = SB_HEADS * SB_HEAD_DIM
MLA_HEADS = 8
MLA_NOPE = 64
MLA_ROPE = 32
MLA_V = 64
MLA_WIDTH = MLA_HEADS * MLA_V
Q_LORA = 256
KV_LORA = 128
ROPE_BASE = 10000.0
Q_BLOCK = 128
SPLITS = (SB_WIDTH, 2 * SB_WIDTH, 3 * SB_WIDTH, 3 * SB_WIDTH + Q_LORA, 3 * SB_WIDTH + Q_LORA + KV_LORA)
IN_COLS = 3 * SB_WIDTH + Q_LORA + KV_LORA + MLA_ROPE
N_GROUPS = 4
EXPERTS_PER_GROUP = 8
N_EXPERTS = N_GROUPS * EXPERTS_PER_GROUP
TOP_K = 2
D_EXPERT = 256
EPS = 1e-6

kernel_name = "hybrid_sb_mla_hmoe_layer"


def _rmsnorm(x, g):
    x32 = x.astype(jnp.float32)
    y = x32 * lax.rsqrt(jnp.mean(x32 * x32, axis=-1, keepdims=True) + EPS)
    return y.astype(x.dtype) * g


def _rope(x, pos):
    half = x.shape[-1] // 2
    inv_freq = ROPE_BASE ** (-jnp.arange(half, dtype=jnp.float32) / half)
    ang = pos.astype(jnp.float32)[..., None] * inv_freq
    cos = jnp.cos(ang)[:, :, None, :].astype(x.dtype)
    sin = jnp.sin(ang)[:, :, None, :].astype(x.dtype)
    x1, x2 = x[..., :half], x[..., half:]
    return jnp.concatenate([x1 * cos - x2 * sin, x2 * cos + x1 * sin], axis=-1)


def _stick_breaking_attention(q, k, v):
    S = q.shape[2]
    scale = SB_HEAD_DIM ** -0.5
    outs = []
    for i in range(S // Q_BLOCK):
        q0 = i * Q_BLOCK
        kv_len = q0 + Q_BLOCK
        z = jnp.einsum('bhqd,bhkd->bhqk', q[:, :, q0:kv_len], k[:, :, :kv_len]).astype(jnp.float32) * scale
        t_idx = q0 + jnp.arange(Q_BLOCK)[:, None]
        s_idx = jnp.arange(kv_len)[None, :]
        strict = s_idx < t_idx
        log1m = jnp.where(strict, -jax.nn.softplus(z), 0.0)
        after = lax.cumsum(log1m, axis=log1m.ndim - 1, reverse=True) - log1m
        a = jnp.where(strict, jnp.exp(jax.nn.log_sigmoid(z) + after), 0.0)
        outs.append(jnp.einsum('bhqk,bhkd->bhqd', a.astype(v.dtype), v[:, :, :kv_len]))
    return jnp.concatenate(outs, axis=2)


def _causal_softmax_attention(q, k, v):
    S = q.shape[2]
    scale = q.shape[-1] ** -0.5
    outs = []
    for i in range(S // Q_BLOCK):
        q0 = i * Q_BLOCK
        kv_len = q0 + Q_BLOCK
        s = jnp.einsum('bhqd,bhkd->bhqk', q[:, :, q0:kv_len], k[:, :, :kv_len]).astype(jnp.float32) * scale
        causal = jnp.arange(kv_len)[None, :] <= (q0 + jnp.arange(Q_BLOCK)[:, None])
        p = jax.nn.softmax(jnp.where(causal, s, -jnp.inf), axis=-1)
        outs.append(jnp.einsum('bhqk,bhkd->bhqd', p.astype(v.dtype), v[:, :, :kv_len]))
    return jnp.concatenate(outs, axis=2)


def _mixer(h, positions, w_in, q_norm, w_uq, kv_norm, w_ukv, sb_out_norm, mla_out_norm, w_out):
    B, S, _ = h.shape
    proj = h @ w_in
    q_sb, k_sb, v_sb, c_q, c_kv, k_r = jnp.split(proj, SPLITS, axis=-1)
    heads = lambda t: t.reshape(B, S, SB_HEADS, SB_HEAD_DIM).transpose(0, 2, 1, 3)
    o_sb = _stick_breaking_attention(heads(q_sb), heads(k_sb), heads(v_sb))
    o_sb = o_sb.transpose(0, 2, 1, 3).reshape(B, S, SB_WIDTH)
    q = (_rmsnorm(c_q, q_norm) @ w_uq).reshape(B, S, MLA_HEADS, MLA_NOPE + MLA_ROPE)
    q = jnp.concatenate([q[..., :MLA_NOPE], _rope(q[..., MLA_NOPE:], positions)], axis=-1)
    kv = (_rmsnorm(c_kv, kv_norm) @ w_ukv).reshape(B, S, MLA_HEADS, MLA_NOPE + MLA_V)
    k_rope = jnp.broadcast_to(_rope(k_r[:, :, None, :], positions), (B, S, MLA_HEADS, MLA_ROPE))
    k = jnp.concatenate([kv[..., :MLA_NOPE], k_rope], axis=-1)
    v = kv[..., MLA_NOPE:]
    o_mla = _causal_softmax_attention(q.transpose(0, 2, 1, 3), k.transpose(0, 2, 1, 3), v.transpose(0, 2, 1, 3))
    o_mla = o_mla.transpose(0, 2, 1, 3).reshape(B, S, MLA_WIDTH)
    o = jnp.concatenate([_rmsnorm(o_sb, sb_out_norm), _rmsnorm(o_mla, mla_out_norm)], axis=-1)
    return o @ w_out


def _hierarchical_moe(h, w_group_router, b_group_router, w_expert_router, b_expert_router, w_gate, w_up, w_down):
    B, S, D = h.shape
    T = B * S
    tok = h.reshape(T, D)
    p_group = jax.nn.softmax((tok @ w_group_router + b_group_router).astype(jnp.float32), axis=-1)
    g_val, g_idx = lax.top_k(p_group, 1)
    g_val, g_idx = g_val[:, 0], g_idx[:, 0]
    e_logits = (tok @ w_expert_router + b_expert_router).astype(jnp.float32).reshape(T, N_GROUPS, EXPERTS_PER_GROUP)
    sel = jnp.broadcast_to(g_idx[:, None, None], (T, 1, EXPERTS_PER_GROUP))
    local = jnp.take_along_axis(e_logits, sel, axis=1)[:, 0]
    e_val, e_idx = lax.top_k(jax.nn.softmax(local, axis=-1), TOP_K)
    weights = g_val[:, None] * e_val / jnp.sum(e_val, axis=-1, keepdims=True)
    flat_e = (g_idx[:, None] * EXPERTS_PER_GROUP + e_idx).reshape(-1)
    flat_w = weights.reshape(-1)
    flat_tok = jnp.arange(T * TOP_K, dtype=jnp.int32) // TOP_K
    order = jnp.argsort(flat_e)
    tok_sorted = flat_tok[order]
    group_sizes = jnp.bincount(flat_e, length=N_EXPERTS).astype(jnp.int32)
    xs = tok[tok_sorted]
    hid = jax.nn.silu(lax.ragged_dot(xs, w_gate, group_sizes)) * lax.ragged_dot(xs, w_up, group_sizes)
    out = lax.ragged_dot(hid, w_down, group_sizes)
    y = jax.ops.segment_sum(out * flat_w[order][:, None].astype(out.dtype), tok_sorted, num_segments=T)
    return y.reshape(B, S, D).astype(h.dtype)


def setup_inputs(seed: int = 0) -> dict:
    key = jax.random.key(seed)
    ks = jax.random.split(key, 20)
    f32 = jnp.float32
    nrm = lambda k, shape, scale: jax.random.normal(k, shape, f32) * scale
    gain = lambda k, n: 1.0 + 0.02 * jax.random.normal(k, (DEPTH, n), f32)
    return {
        "x": jax.random.normal(ks[0], (BATCH, SEQ, D_MODEL), f32),
        "positions": jnp.broadcast_to(jnp.arange(SEQ, dtype=jnp.int32), (BATCH, SEQ)),
        "attn_norm": gain(ks[1], D_MODEL),
        "w_in": nrm(ks[2], (DEPTH, D_MODEL, IN_COLS), D_MODEL ** -0.5),
        "q_norm": gain(ks[3], Q_LORA),
        "w_uq": nrm(ks[4], (DEPTH, Q_LORA, MLA_HEADS * (MLA_NOPE + MLA_ROPE)), Q_LORA ** -0.5),
        "kv_norm": gain(ks[5], KV_LORA),
        "w_ukv": nrm(ks[6], (DEPTH, KV_LORA, MLA_HEADS * (MLA_NOPE + MLA_V)), KV_LORA ** -0.5),
        "sb_out_norm": gain(ks[7], SB_WIDTH),
        "mla_out_norm": gain(ks[8], MLA_WIDTH),
        "w_out": nrm(ks[9], (DEPTH, D_MIX, D_MODEL), D_MIX ** -0.5),
        "ffn_norm": gain(ks[10], D_MODEL),
        "w_group_router": nrm(ks[11], (DEPTH, D_MODEL, N_GROUPS), D_MODEL ** -0.5),
        "b_group_router": nrm(ks[12], (DEPTH, N_GROUPS), 0.01),
        "w_expert_router": nrm(ks[13], (DEPTH, D_MODEL, N_EXPERTS), D_MODEL ** -0.5),
        "b_expert_router": nrm(ks[14], (DEPTH, N_EXPERTS), 0.01),
        "w_gate": nrm(ks[15], (DEPTH, N_EXPERTS, D_MODEL, D_EXPERT), D_MODEL ** -0.5),
        "w_up": nrm(ks[16], (DEPTH, N_EXPERTS, D_MODEL, D_EXPERT), D_MODEL ** -0.5),
        "w_down": nrm(ks[17], (DEPTH, N_EXPERTS, D_EXPERT, D_MODEL), D_EXPERT ** -0.5),
        "final_norm": 1.0 + 0.02 * jax.random.normal(ks[18], (D_MODEL,), f32),
    }


def reference(x, positions, attn_norm, w_in, q_norm, w_uq, kv_norm, w_ukv, sb_out_norm, mla_out_norm,
              w_out, ffn_norm, w_group_router, b_group_router, w_expert_router, b_expert_router,
              w_gate, w_up, w_down, final_norm):
    for l in range(DEPTH):
        h = _rmsnorm(x, attn_norm[l])
        x = x + _mixer(h, positions, w_in[l], q_norm[l], w_uq[l], kv_norm[l], w_ukv[l],
                       sb_out_norm[l], mla_out_norm[l], w_out[l])
        h = _rmsnorm(x, ffn_norm[l])
        x = x + _hierarchical_moe(h, w_group_router[l], b_group_router[l], w_expert_router[l],
                                  b_expert_router[l], w_gate[l], w_up[l], w_down[l])
    return _rmsnorm(x, final_norm)
```

```python
import functools

import jax
import jax.numpy as jnp
import numpy as np
from jax import lax
from jax.experimental import pallas as pl
from jax.experimental.pallas import tpu as pltpu

F32 = jnp.float32
BF16 = jnp.bfloat16

D_MODEL = 1024
SB_HEADS = 8
SB_HEAD_DIM = 64
SB_WIDTH = SB_HEADS * SB_HEAD_DIM
MLA_HEADS = 8
MLA_NOPE = 64
MLA_ROPE = 32
MLA_V = 64
MLA_WIDTH = MLA_HEADS * MLA_V
Q_LORA = 256
KV_LORA = 128
ROPE_BASE = 10000.0
N_GROUPS = 4
EXPERTS_PER_GROUP = 8
N_EXPERTS = N_GROUPS * EXPERTS_PER_GROUP
D_EXPERT = 256
EPS = 1e-6

LANES = 128
SUBLANES = 8
N_PAIRS = EXPERTS_PER_GROUP * (EXPERTS_PER_GROUP - 1) // 2
N_BUCKETS = N_GROUPS * N_PAIRS
assert N_BUCKETS <= LANES
ROPE_HALF = MLA_ROPE // 2
HEAD_PAIRS = SB_HEADS // 2
assert SB_HEADS == MLA_HEADS and 2 * SB_HEAD_DIM == LANES and 2 * MLA_V == LANES

ROW_TILE = 512
ATT_TILE = 256
MOE_TILE = 256
EXT_WIDTH = D_MODEL + LANES
SB_UNDERFLOW = 104.0
VMEM_LIMIT = 56 * 1024 * 1024


def _rms(x):
    return x * lax.rsqrt(jnp.mean(x * x, axis=-1, keepdims=True) + EPS)


def _cparams(*sem):
    return pltpu.CompilerParams(dimension_semantics=sem, vmem_limit_bytes=VMEM_LIMIT)


def _proj_kernel(x_ref, pos_ref, invf_ref, an_ref, wsb_ref, wcq_ref, wckv_ref, wkr_ref, qn_ref, wuq_ref,
                 kvn_ref, wuk_ref, wuv_ref, qsb_ref, ksb_ref, vsb_ref, qm_ref, km_ref, vm_ref):
    hb = (_rms(x_ref[...]) * an_ref[...]).astype(BF16)
    sb = jnp.dot(hb, wsb_ref[...], preferred_element_type=F32)
    qsb_ref[...] = (sb[:, :SB_WIDTH] * (SB_HEAD_DIM ** -0.5)).astype(BF16)
    ksb_ref[...] = sb[:, SB_WIDTH:2 * SB_WIDTH].astype(BF16)
    vsb_ref[...] = sb[:, 2 * SB_WIDTH:].astype(BF16)

    cq = jnp.dot(hb, wcq_ref[...], preferred_element_type=F32)
    ckv = jnp.dot(hb, wckv_ref[...], preferred_element_type=F32)
    kr = jnp.dot(hb, wkr_ref[...], preferred_element_type=F32)
    q = jnp.dot((_rms(cq) * qn_ref[...]).astype(BF16), wuq_ref[...], preferred_element_type=F32)
    ckn = (_rms(ckv) * kvn_ref[...]).astype(BF16)
    kn = jnp.dot(ckn, wuk_ref[...], preferred_element_type=F32)
    vm_ref[...] = jnp.dot(ckn, wuv_ref[...], preferred_element_type=F32).astype(BF16)

    ang = pos_ref[...].astype(F32) * invf_ref[...]
    cos, sin = jnp.cos(ang), jnp.sin(ang)
    lane = lax.broadcasted_iota(jnp.int32, (1, LANES), 1)
    x1_lanes = (lane >= MLA_NOPE) & (lane < MLA_NOPE + ROPE_HALF)
    x2_lanes = (lane >= MLA_NOPE + ROPE_HALF) & (lane < MLA_NOPE + MLA_ROPE)
    c_tab = jnp.where(lane < MLA_NOPE, 1.0, jnp.where(x1_lanes | x2_lanes, cos, 0.0))
    s_from_x2 = jnp.where(x1_lanes, -sin, 0.0)
    s_from_x1 = jnp.where(x2_lanes, sin, 0.0)

    def rope(t):
        return (t * c_tab + pltpu.roll(t, LANES - ROPE_HALF, 1) * s_from_x2
                + pltpu.roll(t, ROPE_HALF, 1) * s_from_x1)

    k_rope = rope(kr)
    q_scale = (MLA_NOPE + MLA_ROPE) ** -0.5
    for h in range(MLA_HEADS):
        blk = slice(h * LANES, (h + 1) * LANES)
        qm_ref[:, blk] = (rope(q[:, blk]) * q_scale).astype(BF16)
        km_ref[:, blk] = (kn[:, blk] + k_rope).astype(BF16)


def _projections(x2d, pos2d, invf, an, w):
    T = x2d.shape[0]
    tm = ROW_TILE
    row = lambda n: pl.BlockSpec((tm, n), lambda i: (i, 0))
    full = lambda a: pl.BlockSpec(a.shape, lambda i: (0,) * a.ndim)
    ins = [x2d, pos2d, invf, an, w["sb"], w["cq"], w["ckv"], w["kr"], w["qn"], w["uq"], w["kvn"], w["uk"], w["uv"]]
    in_specs = [row(D_MODEL), row(1)] + [full(a) for a in ins[2:]]
    widths = (SB_WIDTH, SB_WIDTH, SB_WIDTH, MLA_HEADS * LANES, MLA_HEADS * LANES, MLA_WIDTH)
    return pl.pallas_call(
        _proj_kernel,
        grid=(T // tm,),
        in_specs=in_specs,
        out_specs=[row(n) for n in widths],
        out_shape=[jax.ShapeDtypeStruct((T, n), BF16) for n in widths],
        compiler_params=_cparams("parallel"),
        name="proj",
    )(*ins)


def _softplus(z):
    return jnp.maximum(z, 0.0) + jnp.log1p(jnp.exp(-jnp.abs(z)))


def _sb_kernel(q_ref, k_ref, v_ref, o_ref, acc_ref, c_ref):
    tq = tk = ATT_TILE
    i = pl.program_id(2)
    lane = lax.broadcasted_iota(jnp.int32, (1, LANES), 1)
    row = lax.broadcasted_iota(jnp.int32, (tq, tk), 0)
    col = lax.broadcasted_iota(jnp.int32, (tq, tk), 1)
    later = (row > col).astype(BF16)
    q = q_ref[...]
    outs = []
    for hh in range(2):
        in_head = (lane < SB_HEAD_DIM) if hh == 0 else (lane >= SB_HEAD_DIM)
        qh = jnp.where(in_head, q, jnp.zeros_like(q))
        acc_ref[...] = jnp.zeros_like(acc_ref)
        c_ref[...] = jnp.zeros_like(c_ref)

        def cond(carry):
            j, c_min = carry
            return (j >= 0) & (c_min < SB_UNDERFLOW)

        def body(carry):
            j, _ = carry
            start = pl.multiple_of(j * tk, tk)
            k = k_ref[pl.ds(start, tk), :]
            v = v_ref[pl.ds(start, tk), :]
            z = lax.dot_general(qh, k, (((1,), (1,)), ((), ())), preferred_element_type=F32)
            strict = (col + j * tk) < (row + i * tq)
            sp = jnp.where(strict, _softplus(z), 0.0)
            suffix = jnp.dot(sp.astype(BF16), later, preferred_element_type=F32)
            c = c_ref[...]
            a = jnp.where(strict, jnp.exp(z - sp - suffix - c), 0.0)
            acc_ref[...] += jnp.dot(a.astype(BF16), v, preferred_element_type=F32)
            c_new = c + jnp.sum(sp, axis=-1, keepdims=True)
            c_ref[...] = c_new
            return j - 1, jnp.min(c_new)

        lax.while_loop(cond, body, (i, jnp.float32(0.0)))
        outs.append(acc_ref[...])
    o_ref[...] = jnp.where(lane < SB_HEAD_DIM, outs[0], outs[1])


def _sb_attention(q, k, v, B, S):
    tq = ATT_TILE
    return pl.pallas_call(
        _sb_kernel,
        grid=(B, HEAD_PAIRS, S // tq),
        in_specs=[
            pl.BlockSpec((tq, LANES), lambda b, p, i: (b * (S // tq) + i, p)),
            pl.BlockSpec((S, LANES), lambda b, p, i: (b, p)),
            pl.BlockSpec((S, LANES), lambda b, p, i: (b, p)),
        ],
        out_specs=pl.BlockSpec((tq, LANES), lambda b, p, i: (b * (S // tq) + i, p)),
        out_shape=jax.ShapeDtypeStruct((B * S, SB_WIDTH), F32),
        scratch_shapes=[pltpu.VMEM((tq, LANES), F32), pltpu.VMEM((tq, 1), F32)],
        compiler_params=_cparams("parallel", "parallel", "parallel"),
        name="sb_attn",
    )(q, k, v)


def _mla_kernel(q_ref, k_ref, v_ref, o_ref):
    tq = tk = ATT_TILE
    i = pl.program_id(2)
    lane = lax.broadcasted_iota(jnp.int32, (1, LANES), 1)
    row = lax.broadcasted_iota(jnp.int32, (tq, tk), 0)
    col = lax.broadcasted_iota(jnp.int32, (tq, tk), 1)
    outs = []
    for hh in range(2):
        hl = slice(hh * LANES, (hh + 1) * LANES)
        qh = q_ref[:, hl]

        def block(j, carry, masked):
            m, l, acc = carry
            start = pl.multiple_of(j * tk, tk)
            k = k_ref[pl.ds(start, tk), hl]
            v = v_ref[pl.ds(start, tk), :]
            s = lax.dot_general(qh, k, (((1,), (1,)), ((), ())), preferred_element_type=F32)
            if masked:
                s = jnp.where(col <= row, s, -jnp.inf)
            m_new = jnp.maximum(m, jnp.max(s, axis=-1, keepdims=True))
            alpha = jnp.exp(m - m_new)
            p = jnp.exp(s - m_new)
            l = alpha * l + jnp.sum(p, axis=-1, keepdims=True)
            acc = alpha * acc + jnp.dot(p.astype(BF16), v, preferred_element_type=F32)
            return m_new, l, acc

        init = (jnp.full((tq, 1), -jnp.inf, F32), jnp.zeros((tq, 1), F32), jnp.zeros((tq, LANES), F32))
        carry = lax.fori_loop(0, i, functools.partial(block, masked=False), init)
        _, l, acc = block(i, carry, masked=True)
        outs.append(acc / l)
    o_ref[...] = jnp.where(lane < MLA_V, outs[0], outs[1])


def _mla_attention(q, k, v, B, S):
    tq = ATT_TILE
    return pl.pallas_call(
        _mla_kernel,
        grid=(B, HEAD_PAIRS, S // tq),
        in_specs=[
            pl.BlockSpec((tq, 2 * LANES), lambda b, p, i: (b * (S // tq) + i, p)),
            pl.BlockSpec((S, 2 * LANES), lambda b, p, i: (b, p)),
            pl.BlockSpec((S, LANES), lambda b, p, i: (b, p)),
        ],
        out_specs=pl.BlockSpec((tq, LANES), lambda b, p, i: (b * (S // tq) + i, p)),
        out_shape=jax.ShapeDtypeStruct((B * S, MLA_WIDTH), F32),
        compiler_params=_cparams("parallel", "parallel", "parallel"),
        name="mla_attn",
    )(q, k, v)


ROUTER_ROWS = SUBLANES * (1 + N_GROUPS)


def _split_bf16(a):
    hi = a.astype(BF16)
    return hi, (a - hi.astype(F32)).astype(BF16)


def _route_kernel(osb_ref, omla_ref, x_ref, g_sb_ref, g_mla_ref, wout_ref, fn_ref, wr_hi_ref, wr_lo_ref, br_ref,
                  x1_ref, hext_ref, bucket_ref, rank_ref, counts_ref, carry_ref):
    tm = ROW_TILE

    @pl.when(pl.program_id(0) == 0)
    def _():
        carry_ref[...] = jnp.zeros_like(carry_ref)

    o = jnp.concatenate([_rms(osb_ref[...]) * g_sb_ref[...], _rms(omla_ref[...]) * g_mla_ref[...]], axis=-1)
    x1 = x_ref[...] + jnp.dot(o.astype(BF16), wout_ref[...], preferred_element_type=F32)
    x1_ref[...] = x1
    h = _rms(x1) * fn_ref[...]
    hext_ref[:, :D_MODEL] = h

    h_hi, h_lo = _split_bf16(h)
    logits = (jnp.dot(h_hi, wr_hi_ref[...], preferred_element_type=F32)
              + jnp.dot(h_hi, wr_lo_ref[...], preferred_element_type=F32)
              + jnp.dot(h_lo, wr_hi_ref[...], preferred_element_type=F32)) + br_ref[...]
    lt = logits.T

    rid = lax.broadcasted_iota(jnp.int32, (SUBLANES, tm), 0)
    g_logit = jnp.where(rid < N_GROUPS, lt[:SUBLANES], -jnp.inf)
    g_exp = jnp.exp(g_logit - jnp.max(g_logit, axis=0, keepdims=True))
    p_group = g_exp / jnp.sum(g_exp, axis=0, keepdims=True)
    g_val = jnp.max(p_group, axis=0, keepdims=True)
    g_idx = jnp.min(jnp.where(p_group == g_val, rid, SUBLANES), axis=0, keepdims=True)
    local = lt[SUBLANES * N_GROUPS:SUBLANES * (N_GROUPS + 1)]
    for g in range(N_GROUPS - 2, -1, -1):
        local = jnp.where(g_idx == g, lt[SUBLANES * (g + 1):SUBLANES * (g + 2)], local)
    e_exp = jnp.exp(local - jnp.max(local, axis=0, keepdims=True))
    p_exp = e_exp / jnp.sum(e_exp, axis=0, keepdims=True)
    v1 = jnp.max(p_exp, axis=0, keepdims=True)
    i1 = jnp.min(jnp.where(p_exp == v1, rid, SUBLANES), axis=0, keepdims=True)
    rest = jnp.where(rid == i1, -1.0, p_exp)
    v2 = jnp.max(rest, axis=0, keepdims=True)
    i2 = jnp.min(jnp.where(rest == v2, rid, SUBLANES), axis=0, keepdims=True)
    den = v1 + v2
    w1 = g_val * v1 / den
    w2 = g_val * v2 / den
    first_lower = i1 < i2
    e_lo = jnp.where(first_lower, i1, i2)
    e_hi = jnp.where(first_lower, i2, i1)
    w_lo = jnp.where(first_lower, w1, w2)
    w_hi = jnp.where(first_lower, w2, w1)
    pair = ((e_lo * (2 * EXPERTS_PER_GROUP - 1 - e_lo)) >> 1) + (e_hi - e_lo - 1)
    bucket = g_idx * N_PAIRS + pair
    bucket_ref[0] = bucket

    rid_full = lax.broadcasted_iota(jnp.int32, (LANES, tm), 0)
    w_rows = jnp.where(rid_full == 0, w_lo, jnp.where(rid_full == 1, w_hi, 0.0))
    hext_ref[:, D_MODEL:] = w_rows.T

    onehot = (rid_full == bucket).astype(F32)
    trow = lax.broadcasted_iota(jnp.int32, (tm, tm), 0)
    tcol = lax.broadcasted_iota(jnp.int32, (tm, tm), 1)
    earlier = (trow < tcol).astype(BF16)
    before = jnp.dot(onehot.astype(BF16), earlier, preferred_element_type=F32) + carry_ref[...]
    rank_ref[0] = jnp.sum(onehot * before, axis=0, keepdims=True).astype(jnp.int32)
    carry_ref[...] += jnp.sum(onehot, axis=1, keepdims=True)
    counts_ref[...] = carry_ref[...]


def _route(o_sb, o_mla, x2d, g_sb, g_mla, w_out, fn, wr_hi, wr_lo, br):
    T = x2d.shape[0]
    tm = ROW_TILE
    nt = T // tm
    row = lambda n: pl.BlockSpec((tm, n), lambda i: (i, 0))
    full = lambda a: pl.BlockSpec(a.shape, lambda i: (0,) * a.ndim)
    tok = pl.BlockSpec((1, 1, tm), lambda i: (i, 0, 0))
    ins = [o_sb, o_mla, x2d, g_sb, g_mla, w_out, fn, wr_hi, wr_lo, br]
    return pl.pallas_call(
        _route_kernel,
        grid=(nt,),
        in_specs=[row(SB_WIDTH), row(MLA_WIDTH), row(D_MODEL)] + [full(a) for a in ins[3:]],
        out_specs=[row(D_MODEL), row(EXT_WIDTH), tok, tok, pl.BlockSpec((LANES, 1), lambda i: (0, 0))],
        out_shape=[
            jax.ShapeDtypeStruct((T, D_MODEL), F32),
            jax.ShapeDtypeStruct((T, EXT_WIDTH), F32),
            jax.ShapeDtypeStruct((nt, 1, tm), jnp.int32),
            jax.ShapeDtypeStruct((nt, 1, tm), jnp.int32),
            jax.ShapeDtypeStruct((LANES, 1), F32),
        ],
        scratch_shapes=[pltpu.VMEM((LANES, 1), F32)],
        compiler_params=_cparams("arbitrary"),
        name="route",
    )(*ins)


def _dest_kernel(bucket_ref, rank_ref, offs_ref, dest_ref):
    tm = bucket_ref.shape[-1]
    rid = lax.broadcasted_iota(jnp.int32, (LANES, tm), 0)
    start = jnp.sum(jnp.where(rid == bucket_ref[0], offs_ref[...], 0), axis=0, keepdims=True)
    dest_ref[0] = start + rank_ref[0]


def _dest_rows(bucket, rank, offsets):
    nt, _, tm = bucket.shape
    tok = pl.BlockSpec((1, 1, tm), lambda i: (i, 0, 0))
    return pl.pallas_call(
        _dest_kernel,
        grid=(nt,),
        in_specs=[tok, tok, pl.BlockSpec((LANES, 1), lambda i: (0, 0))],
        out_specs=tok,
        out_shape=jax.ShapeDtypeStruct((nt, 1, tm), jnp.int32),
        compiler_params=_cparams("parallel"),
        name="dest",
    )(bucket, rank, offsets)


def _scatter_kernel(dest_ref, h_ref, init_ref, xs_ref, sem):
    del init_ref
    tm = h_ref.shape[0]
    base = pl.program_id(0) * tm

    def issue(r, _):
        d = dest_ref[base + r]
        pltpu.make_async_copy(h_ref.at[pl.ds(r, 1)], xs_ref.at[pl.ds(d, 1)], sem).start()
        return _

    lax.fori_loop(0, tm, issue, 0, unroll=8)
    pltpu.make_async_copy(h_ref, xs_ref.at[pl.ds(0, tm)], sem).wait()


def _scatter_rows(dest, hext, n_rows):
    T, W = hext.shape
    tm = ROW_TILE
    init = jnp.zeros((n_rows, W), hext.dtype)
    return pl.pallas_call(
        _scatter_kernel,
        grid_spec=pltpu.PrefetchScalarGridSpec(
            num_scalar_prefetch=1,
            grid=(T // tm,),
            in_specs=[pl.BlockSpec((tm, W), lambda i, d: (i, 0)), pl.BlockSpec(memory_space=pl.ANY)],
            out_specs=pl.BlockSpec(memory_space=pl.ANY),
            scratch_shapes=[pltpu.SemaphoreType.DMA],
        ),
        out_shape=jax.ShapeDtypeStruct((n_rows, W), hext.dtype),
        input_output_aliases={2: 0},
        compiler_params=_cparams("arbitrary"),
        name="scatter",
    )(dest, hext, init)


def _moe_kernel(elo_ref, ehi_ref, nt_ref, xs_ref, wgu_lo_ref, wd_lo_ref, wgu_hi_ref, wd_hi_ref, ys_ref):
    del elo_ref, ehi_ref
    used = pl.program_id(0) < nt_ref[0]

    @pl.when(jnp.logical_not(used))
    def _():
        ys_ref[...] = jnp.zeros_like(ys_ref)

    @pl.when(used)
    def _():
        h = xs_ref[:, :D_MODEL].astype(BF16)
        gates = xs_ref[:, D_MODEL:]

        def expert(wgu_ref, wd_ref):
            gu = jnp.dot(h, wgu_ref[0], preferred_element_type=F32)
            g, u = gu[:, :D_EXPERT], gu[:, D_EXPERT:]
            hid = (g * jax.nn.sigmoid(g)) * u
            return jnp.dot(hid.astype(BF16), wd_ref[0], preferred_element_type=F32)

        ys_ref[...] = (expert(wgu_lo_ref, wd_lo_ref) * gates[:, 0:1]
                       + expert(wgu_hi_ref, wd_hi_ref) * gates[:, 1:2])


def _moe(tile_elo, tile_ehi, n_tiles_used, xs, w_gu, w_d):
    n_rows = xs.shape[0]
    tile = MOE_TILE
    rows = lambda w: pl.BlockSpec((tile, w), lambda i, elo, ehi, nt: (jnp.minimum(i, nt[0] - 1), 0))
    wspec = lambda shape, which: pl.BlockSpec(
        (1,) + shape, lambda i, elo, ehi, nt: ((elo, ehi)[which][i], 0, 0))
    return pl.pallas_call(
        _moe_kernel,
        grid_spec=pltpu.PrefetchScalarGridSpec(
            num_scalar_prefetch=3,
            grid=(n_rows // tile,),
            in_specs=[
                rows(EXT_WIDTH),
                wspec((D_MODEL, 2 * D_EXPERT), 0), wspec((D_EXPERT, D_MODEL), 0),
                wspec((D_MODEL, 2 * D_EXPERT), 1), wspec((D_EXPERT, D_MODEL), 1),
            ],
            out_specs=pl.BlockSpec((tile, D_MODEL), lambda i, elo, ehi, nt: (i, 0)),
        ),
        out_shape=jax.ShapeDtypeStruct((n_rows, D_MODEL), F32),
        compiler_params=_cparams("arbitrary"),
        name="moe",
    )(tile_elo, tile_ehi, n_tiles_used, xs, w_gu, w_d, w_gu, w_d)


def _final_kernel(dest_ref, ys_ref, x1_ref, fn_ref, o_ref, buf_ref, sem):
    tm = x1_ref.shape[0]
    base = pl.program_id(0) * tm

    def issue(r, _):
        d = dest_ref[base + r]
        pltpu.make_async_copy(ys_ref.at[pl.ds(d, 1)], buf_ref.at[pl.ds(r, 1)], sem).start()
        return _

    lax.fori_loop(0, tm, issue, 0, unroll=8)
    pltpu.make_async_copy(ys_ref.at[pl.ds(0, tm)], buf_ref, sem).wait()
    o_ref[...] = _rms(x1_ref[...] + buf_ref[...]) * fn_ref[...]


def _final(dest, ys, x1, fn):
    T, D = x1.shape
    tm = ROW_TILE
    return pl.pallas_call(
        _final_kernel,
        grid_spec=pltpu.PrefetchScalarGridSpec(
            num_scalar_prefetch=1,
            grid=(T // tm,),
            in_specs=[
                pl.BlockSpec(memory_space=pl.ANY),
                pl.BlockSpec((tm, D), lambda i, d: (i, 0)),
                pl.BlockSpec((1, D), lambda i, d: (0, 0)),
            ],
            out_specs=pl.BlockSpec((tm, D), lambda i, d: (i, 0)),
            scratch_shapes=[pltpu.VMEM((tm, D), F32), pltpu.SemaphoreType.DMA],
        ),
        out_shape=jax.ShapeDtypeStruct((T, D), F32),
        compiler_params=_cparams("arbitrary"),
        name="final",
    )(dest, ys, x1, fn)


def _pair_tables():
    lo, hi = [], []
    for g in range(N_GROUPS):
        for a in range(EXPERTS_PER_GROUP):
            for b in range(a + 1, EXPERTS_PER_GROUP):
                lo.append(g * EXPERTS_PER_GROUP + a)
                hi.append(g * EXPERTS_PER_GROUP + b)
    return np.asarray(lo, np.int32), np.asarray(hi, np.int32)


def _attention_weights(w_in, q_norm, w_uq, kv_norm, w_ukv):
    D = w_in.shape[0]
    c0 = 3 * SB_WIDTH
    zeros = lambda r, c: jnp.zeros((r, c), F32)
    w_kr = jnp.concatenate(
        [zeros(D, MLA_NOPE), w_in[:, c0 + Q_LORA + KV_LORA:], zeros(D, LANES - MLA_NOPE - MLA_ROPE)], axis=1)
    dq = MLA_NOPE + MLA_ROPE
    uq = jnp.concatenate(
        [jnp.concatenate([w_uq[:, h * dq:(h + 1) * dq], zeros(Q_LORA, LANES - dq)], axis=1)
         for h in range(MLA_HEADS)], axis=1)
    dkv = MLA_NOPE + MLA_V
    uk = jnp.concatenate(
        [jnp.concatenate([w_ukv[:, h * dkv:h * dkv + MLA_NOPE], zeros(KV_LORA, LANES - MLA_NOPE)], axis=1)
         for h in range(MLA_HEADS)], axis=1)
    uv = jnp.concatenate([w_ukv[:, h * dkv + MLA_NOPE:(h + 1) * dkv] for h in range(MLA_HEADS)], axis=1)
    return {
        "sb": w_in[:, :c0].astype(BF16),
        "cq": w_in[:, c0:c0 + Q_LORA].astype(BF16),
        "ckv": w_in[:, c0 + Q_LORA:c0 + Q_LORA + KV_LORA].astype(BF16),
        "kr": w_kr.astype(BF16),
        "qn": q_norm[None, :],
        "uq": uq.astype(BF16),
        "kvn": kv_norm[None, :],
        "uk": uk.astype(BF16),
        "uv": uv.astype(BF16),
    }


def _router_weights(w_group, b_group, w_expert, b_expert):
    D = w_group.shape[0]
    pad_g = SUBLANES - N_GROUPS
    pad_e = LANES - SUBLANES - N_EXPERTS
    w = jnp.concatenate([w_group, jnp.zeros((D, pad_g), F32), w_expert, jnp.zeros((D, pad_e), F32)], axis=1)
    b = jnp.concatenate([b_group, jnp.zeros((pad_g,), F32), b_expert, jnp.zeros((pad_e,), F32)])[None, :]
    w_hi, w_lo = _split_bf16(w)
    return w_hi, w_lo, b


def _bucket_layout(counts, n_tiles):
    c = counts[:N_BUCKETS, 0].astype(jnp.int32)
    tiles = (c + MOE_TILE - 1) // MOE_TILE
    tile_end = jnp.cumsum(tiles)
    offsets = (tile_end - tiles) * MOE_TILE
    offsets = jnp.concatenate([offsets, jnp.zeros((LANES - N_BUCKETS,), jnp.int32)])[:, None]
    n_used = tile_end[-1]
    tile_id = jnp.minimum(jnp.arange(n_tiles, dtype=jnp.int32), n_used - 1)
    tile_bucket = jnp.searchsorted(tile_end, tile_id, side="right").astype(jnp.int32)
    pair_lo, pair_hi = _pair_tables()
    return offsets, jnp.asarray(pair_lo)[tile_bucket], jnp.asarray(pair_hi)[tile_bucket], n_used[None]


def kernel(x, positions, attn_norm, w_in, q_norm, w_uq, kv_norm, w_ukv, sb_out_norm, mla_out_norm, w_out,
           ffn_norm, w_group_router, b_group_router, w_expert_router, b_expert_router, w_gate, w_up, w_down,
           final_norm):
    B, S, D = x.shape
    T = B * S
    depth = w_in.shape[0]
    assert D == D_MODEL and T % ROW_TILE == 0 and S % ATT_TILE == 0
    assert depth == 1, "the final norm is fused into the last layer's gather kernel"
    n_sorted_tiles = T // MOE_TILE + N_BUCKETS
    n_sorted_rows = n_sorted_tiles * MOE_TILE

    lane = jnp.arange(LANES)
    invf = (ROPE_BASE ** (-(lane % ROPE_HALF).astype(F32) / ROPE_HALF))[None, :]
    pos2d = positions.reshape(T, 1)
    x2d = x.reshape(T, D)
    for l in range(depth):
        aw = _attention_weights(w_in[l], q_norm[l], w_uq[l], kv_norm[l], w_ukv[l])
        q_sb, k_sb, v_sb, q_m, k_m, v_m = _projections(x2d, pos2d, invf, attn_norm[l][None, :], aw)
        o_sb = _sb_attention(q_sb, k_sb, v_sb, B, S)
        o_mla = _mla_attention(q_m, k_m, v_m, B, S)

        wr_hi, wr_lo, br = _router_weights(w_group_router[l], b_group_router[l], w_expert_router[l], b_expert_router[l])
        x1, hext, bucket, rank, counts = _route(
            o_sb, o_mla, x2d, sb_out_norm[l][None, :], mla_out_norm[l][None, :], w_out[l].astype(BF16),
            ffn_norm[l][None, :], wr_hi, wr_lo, br)
        offsets, tile_elo, tile_ehi, n_used = _bucket_layout(counts, n_sorted_tiles)
        dest = _dest_rows(bucket, rank, offsets).reshape(T)

        xs = _scatter_rows(dest, hext, n_sorted_rows)
        w_gu = jnp.concatenate([w_gate[l], w_up[l]], axis=-1).astype(BF16)
        ys = _moe(tile_elo, tile_ehi, n_used, xs, w_gu, w_down[l].astype(BF16))
        x2d = _final(dest, ys, x1, final_norm[None, :])
    return x2d.reshape(B, S, D)
```

```python
import functools

import jax
import jax.numpy as jnp
import numpy as np
from jax import lax
from jax.experimental import pallas as pl
from jax.experimental.pallas import tpu as pltpu

F32 = jnp.float32
BF16 = jnp.bfloat16

D_MODEL = 1024
SB_HEADS = 8
SB_HEAD_DIM = 64
SB_WIDTH = SB_HEADS * SB_HEAD_DIM
MLA_HEADS = 8
MLA_NOPE = 64
MLA_ROPE = 32
MLA_V = 64
MLA_WIDTH = MLA_HEADS * MLA_V
Q_LORA = 256
KV_LORA = 128
ROPE_BASE = 10000.0
N_GROUPS = 4
EXPERTS_PER_GROUP = 8
N_EXPERTS = N_GROUPS * EXPERTS_PER_GROUP
D_EXPERT = 256
EPS = 1e-6

LANES = 128
SUBLANES = 8
N_PAIRS = EXPERTS_PER_GROUP * (EXPERTS_PER_GROUP - 1) // 2
N_BUCKETS = N_GROUPS * N_PAIRS
assert N_BUCKETS <= LANES
ROPE_HALF = MLA_ROPE // 2
HEAD_PAIRS = SB_HEADS // 2
assert SB_HEADS == MLA_HEADS and 2 * SB_HEAD_DIM == LANES and 2 * MLA_V == LANES

ROW_TILE = 512
ATT_TILE = 256
ATT_CHUNK = 64
MOE_TILE = 256
EXT_WIDTH = D_MODEL + LANES
SB_UNDERFLOW = 104.0
VMEM_LIMIT = 56 * 1024 * 1024


def _rms(x):
    return x * lax.rsqrt(jnp.mean(x * x, axis=-1, keepdims=True) + EPS)


def _cparams(*sem):
    return pltpu.CompilerParams(dimension_semantics=sem, vmem_limit_bytes=VMEM_LIMIT)


def _proj_kernel(x_ref, pos_ref, invf_ref, an_ref, wsb_ref, wcq_ref, wckv_ref, wkr_ref, qn_ref, wuq_ref,
                 kvn_ref, wuk_ref, wuv_ref, qsb_ref, ksb_ref, vsb_ref, qm_ref, km_ref, vm_ref):
    hb = (_rms(x_ref[...]) * an_ref[...]).astype(BF16)
    sb = jnp.dot(hb, wsb_ref[...], preferred_element_type=F32)
    qsb_ref[...] = (sb[:, :SB_WIDTH] * (SB_HEAD_DIM ** -0.5)).astype(BF16)
    ksb_ref[...] = sb[:, SB_WIDTH:2 * SB_WIDTH].astype(BF16)
    vsb_ref[...] = sb[:, 2 * SB_WIDTH:].astype(BF16)

    cq = jnp.dot(hb, wcq_ref[...], preferred_element_type=F32)
    ckv = jnp.dot(hb, wckv_ref[...], preferred_element_type=F32)
    kr = jnp.dot(hb, wkr_ref[...], preferred_element_type=F32)
    q = jnp.dot((_rms(cq) * qn_ref[...]).astype(BF16), wuq_ref[...], preferred_element_type=F32)
    ckn = (_rms(ckv) * kvn_ref[...]).astype(BF16)
    kn = jnp.dot(ckn, wuk_ref[...], preferred_element_type=F32)
    v = jnp.dot(ckn, wuv_ref[...], preferred_element_type=F32)

    ang = pos_ref[...].astype(F32) * invf_ref[...]
    cos, sin = jnp.cos(ang), jnp.sin(ang)
    lane = lax.broadcasted_iota(jnp.int32, (1, LANES), 1)
    x1_lanes = (lane >= MLA_NOPE) & (lane < MLA_NOPE + ROPE_HALF)
    x2_lanes = (lane >= MLA_NOPE + ROPE_HALF) & (lane < MLA_NOPE + MLA_ROPE)
    c_tab = jnp.where(lane < MLA_NOPE, 1.0, jnp.where(x1_lanes | x2_lanes, cos, 0.0))
    s_from_x2 = jnp.where(x1_lanes, -sin, 0.0)
    s_from_x1 = jnp.where(x2_lanes, sin, 0.0)

    def rope(t):
        return (t * c_tab + pltpu.roll(t, LANES - ROPE_HALF, 1) * s_from_x2
                + pltpu.roll(t, ROPE_HALF, 1) * s_from_x1)

    k_rope = rope(kr)
    q_scale = (MLA_NOPE + MLA_ROPE) ** -0.5
    for h in range(MLA_HEADS):
        blk = slice(h * LANES, (h + 1) * LANES)
        qm_ref[:, blk] = (rope(q[:, blk]) * q_scale).astype(BF16)
        km_ref[:, blk] = (kn[:, blk] + k_rope).astype(BF16)
        pair = v[:, (h // 2) * LANES:(h // 2 + 1) * LANES]
        own = (lane < MLA_V) if h % 2 == 0 else (lane >= MLA_V)
        vm_ref[:, blk] = jnp.where(own, pair, 1.0).astype(BF16)


def _projections(x2d, pos2d, invf, an, w):
    T = x2d.shape[0]
    tm = ROW_TILE
    row = lambda n: pl.BlockSpec((tm, n), lambda i: (i, 0))
    full = lambda a: pl.BlockSpec(a.shape, lambda i: (0,) * a.ndim)
    ins = [x2d, pos2d, invf, an, w["sb"], w["cq"], w["ckv"], w["kr"], w["qn"], w["uq"], w["kvn"], w["uk"], w["uv"]]
    in_specs = [row(D_MODEL), row(1)] + [full(a) for a in ins[2:]]
    widths = (SB_WIDTH, SB_WIDTH, SB_WIDTH, MLA_HEADS * LANES, MLA_HEADS * LANES, MLA_HEADS * LANES)
    return pl.pallas_call(
        _proj_kernel,
        grid=(T // tm,),
        in_specs=in_specs,
        out_specs=[row(n) for n in widths],
        out_shape=[jax.ShapeDtypeStruct((T, n), BF16) for n in widths],
        compiler_params=_cparams("parallel"),
        name="proj",
    )(*ins)


def _softplus(z):
    return jnp.maximum(z, 0.0) + jnp.log1p(jnp.exp(-jnp.abs(z)))


def _sb_kernel(q_ref, k_ref, v_ref, o_ref, z_ref, w_ref, acc_ref, c_ref):
    tq = tk = ATT_TILE
    ch = ATT_CHUNK
    i = pl.program_id(1)
    lane = lax.broadcasted_iota(jnp.int32, (1, LANES), 1)
    in_head = (lane < SB_HEAD_DIM, lane >= SB_HEAD_DIM)
    later = (lax.broadcasted_iota(jnp.int32, (tk, tk), 0)
             > lax.broadcasted_iota(jnp.int32, (tk, tk), 1)).astype(BF16)
    strict = [lax.broadcasted_iota(jnp.int32, (ch, tk), 1) < lax.broadcasted_iota(jnp.int32, (ch, tk), 0) + r * ch
              for r in range(tq // ch)]
    nt_dims = (((1,), (1,)), ((), ()))
    pair_lanes = lambda h: slice((h // 2) * LANES, (h // 2 + 1) * LANES)

    def visit(j, diagonal):
        keys = pl.ds(pl.multiple_of(j * tk, tk), tk)
        for h in range(SB_HEADS):
            q_pair = q_ref[:, pair_lanes(h)]
            qh = jnp.where(in_head[h % 2], q_pair, jnp.zeros_like(q_pair))
            z_ref[h] = lax.dot_general(qh, k_ref[keys, pair_lanes(h)], nt_dims, preferred_element_type=F32)
        c_low = None
        for h in range(SB_HEADS):
            for r in range(tq // ch):
                rows = slice(r * ch, (r + 1) * ch)
                z = z_ref[h, rows]
                sp = _softplus(z)
                if diagonal:
                    sp = jnp.where(strict[r], sp, 0.0)
                    c_new = jnp.sum(sp, axis=-1, keepdims=True)
                    z_ref[h, rows] = z - sp
                else:
                    c_old = c_ref[h, rows]
                    c_new = c_old + jnp.sum(sp, axis=-1, keepdims=True)
                    z_ref[h, rows] = (z - sp) - c_old
                w_ref[h, rows] = sp.astype(BF16)
                c_ref[h, rows] = c_new
                c_low = c_new if c_low is None else jnp.minimum(c_low, c_new)
        for h in range(SB_HEADS):
            suffix = jnp.dot(w_ref[h], later, preferred_element_type=F32)
            for r in range(tq // ch):
                rows = slice(r * ch, (r + 1) * ch)
                a = jnp.exp(z_ref[h, rows] - suffix[rows])
                if diagonal:
                    a = jnp.where(strict[r], a, 0.0)
                w_ref[h, rows] = a.astype(BF16)
        for h in range(SB_HEADS):
            av = jnp.dot(w_ref[h], v_ref[keys, pair_lanes(h)], preferred_element_type=F32)
            if diagonal:
                acc_ref[h] = av
            else:
                acc_ref[h] += av
        return jnp.min(c_low)

    def cond(carry):
        j, c_min = carry
        return (j >= 0) & (c_min < SB_UNDERFLOW)

    def body(carry):
        j, _ = carry
        return j - 1, visit(j, diagonal=False)

    lax.while_loop(cond, body, (i - 1, visit(i, diagonal=True)))
    for p in range(HEAD_PAIRS):
        o_ref[:, p * LANES:(p + 1) * LANES] = jnp.where(in_head[0], acc_ref[2 * p], acc_ref[2 * p + 1])


def _sb_attention(q, k, v, B, S):
    tq = tk = ATT_TILE
    nq = S // tq
    return pl.pallas_call(
        _sb_kernel,
        grid=(B, nq),
        in_specs=[
            pl.BlockSpec((tq, SB_WIDTH), lambda b, i: (b * nq + i, 0)),
            pl.BlockSpec((S, SB_WIDTH), lambda b, i: (b, 0)),
            pl.BlockSpec((S, SB_WIDTH), lambda b, i: (b, 0)),
        ],
        out_specs=pl.BlockSpec((tq, SB_WIDTH), lambda b, i: (b * nq + i, 0)),
        out_shape=jax.ShapeDtypeStruct((B * S, SB_WIDTH), F32),
        scratch_shapes=[
            pltpu.VMEM((SB_HEADS, tq, tk), F32),
            pltpu.VMEM((SB_HEADS, tq, tk), BF16),
            pltpu.VMEM((SB_HEADS, tq, LANES), F32),
            pltpu.VMEM((SB_HEADS, tq, 1), F32),
        ],
        compiler_params=_cparams("parallel", "parallel"),
        name="sb_attn",
    )(q, k, v)


def _mla_kernel(q_ref, k_ref, v_ref, o_ref, s_ref, p_ref, acc_ref, m_ref, alpha_ref):
    tq = tk = ATT_TILE
    ch = ATT_CHUNK
    i = pl.program_id(1)
    lane = lax.broadcasted_iota(jnp.int32, (1, LANES), 1)
    causal = [lax.broadcasted_iota(jnp.int32, (ch, tk), 1) <= lax.broadcasted_iota(jnp.int32, (ch, tk), 0) + r * ch
              for r in range(tq // ch)]
    nt_dims = (((1,), (1,)), ((), ()))
    head_lanes = lambda h: slice(h * LANES, (h + 1) * LANES)

    def visit(j, diagonal):
        keys = pl.ds(pl.multiple_of(j * tk, tk), tk)
        for h in range(MLA_HEADS):
            s_ref[h] = lax.dot_general(q_ref[:, head_lanes(h)], k_ref[keys, head_lanes(h)], nt_dims,
                                       preferred_element_type=F32)
        for h in range(MLA_HEADS):
            for r in range(tq // ch):
                rows = slice(r * ch, (r + 1) * ch)
                s = s_ref[h, rows]
                if diagonal:
                    s = jnp.where(causal[r], s, -jnp.inf)
                    m_new = jnp.max(s, axis=-1, keepdims=True)
                else:
                    m_old = m_ref[h, rows]
                    m_new = jnp.maximum(m_old, jnp.max(s, axis=-1, keepdims=True))
                    alpha_ref[h, rows] = jnp.exp(m_old - m_new)
                p_ref[h, rows] = jnp.exp(s - m_new).astype(BF16)
                m_ref[h, rows] = m_new
        for h in range(MLA_HEADS):
            pv = jnp.dot(p_ref[h], v_ref[keys, head_lanes(h)], preferred_element_type=F32)
            if diagonal:
                acc_ref[h] = pv
            else:
                acc_ref[h] = alpha_ref[h] * acc_ref[h] + pv

    visit(i, diagonal=True)

    def body(j, carry):
        visit(j, diagonal=False)
        return carry

    lax.fori_loop(0, i, body, 0)
    for p in range(HEAD_PAIRS):
        outs = [acc_ref[2 * p + e] / pltpu.roll(acc_ref[2 * p + e], MLA_V, 1) for e in range(2)]
        o_ref[:, p * LANES:(p + 1) * LANES] = jnp.where(lane < MLA_V, outs[0], outs[1])


def _mla_attention(q, k, v, B, S):
    tq = tk = ATT_TILE
    nq = S // tq
    width = MLA_HEADS * LANES
    return pl.pallas_call(
        _mla_kernel,
        grid=(B, nq),
        in_specs=[
            pl.BlockSpec((tq, width), lambda b, i: (b * nq + i, 0)),
            pl.BlockSpec((S, width), lambda b, i: (b, 0)),
            pl.BlockSpec((S, width), lambda b, i: (b, 0)),
        ],
        out_specs=pl.BlockSpec((tq, MLA_WIDTH), lambda b, i: (b * nq + i, 0)),
        out_shape=jax.ShapeDtypeStruct((B * S, MLA_WIDTH), F32),
        scratch_shapes=[
            pltpu.VMEM((MLA_HEADS, tq, tk), F32),
            pltpu.VMEM((MLA_HEADS, tq, tk), BF16),
            pltpu.VMEM((MLA_HEADS, tq, LANES), F32),
            pltpu.VMEM((MLA_HEADS, tq, 1), F32),
            pltpu.VMEM((MLA_HEADS, tq, 1), F32),
        ],
        compiler_params=_cparams("parallel", "parallel"),
        name="mla_attn",
    )(q, k, v)


ROUTER_ROWS = SUBLANES * (1 + N_GROUPS)


def _split_bf16(a):
    hi = a.astype(BF16)
    return hi, (a - hi.astype(F32)).astype(BF16)


def _route_kernel(osb_ref, omla_ref, x_ref, g_sb_ref, g_mla_ref, wout_ref, fn_ref, wr_hi_ref, wr_lo_ref, br_ref,
                  x1_ref, hext_ref, bucket_ref, rank_ref, counts_ref, carry_ref):
    tm = ROW_TILE

    @pl.when(pl.program_id(0) == 0)
    def _():
        carry_ref[...] = jnp.zeros_like(carry_ref)

    o = jnp.concatenate([_rms(osb_ref[...]) * g_sb_ref[...], _rms(omla_ref[...]) * g_mla_ref[...]], axis=-1)
    x1 = x_ref[...] + jnp.dot(o.astype(BF16), wout_ref[...], preferred_element_type=F32)
    x1_ref[...] = x1
    h = _rms(x1) * fn_ref[...]
    hext_ref[:, :D_MODEL] = h

    h_hi, h_lo = _split_bf16(h)
    logits = (jnp.dot(h_hi, wr_hi_ref[...], preferred_element_type=F32)
              + jnp.dot(h_hi, wr_lo_ref[...], preferred_element_type=F32)
              + jnp.dot(h_lo, wr_hi_ref[...], preferred_element_type=F32)) + br_ref[...]
    lt = logits.T

    rid = lax.broadcasted_iota(jnp.int32, (SUBLANES, tm), 0)
    g_logit = jnp.where(rid < N_GROUPS, lt[:SUBLANES], -jnp.inf)
    g_exp = jnp.exp(g_logit - jnp.max(g_logit, axis=0, keepdims=True))
    p_group = g_exp / jnp.sum(g_exp, axis=0, keepdims=True)
    g_val = jnp.max(p_group, axis=0, keepdims=True)
    g_idx = jnp.min(jnp.where(p_group == g_val, rid, SUBLANES), axis=0, keepdims=True)
    local = lt[SUBLANES * N_GROUPS:SUBLANES * (N_GROUPS + 1)]
    for g in range(N_GROUPS - 2, -1, -1):
        local = jnp.where(g_idx == g, lt[SUBLANES * (g + 1):SUBLANES * (g + 2)], local)
    e_exp = jnp.exp(local - jnp.max(local, axis=0, keepdims=True))
    p_exp = e_exp / jnp.sum(e_exp, axis=0, keepdims=True)
    v1 = jnp.max(p_exp, axis=0, keepdims=True)
    i1 = jnp.min(jnp.where(p_exp == v1, rid, SUBLANES), axis=0, keepdims=True)
    rest = jnp.where(rid == i1, -1.0, p_exp)
    v2 = jnp.max(rest, axis=0, keepdims=True)
    i2 = jnp.min(jnp.where(rest == v2, rid, SUBLANES), axis=0, keepdims=True)
    den = v1 + v2
    w1 = g_val * v1 / den
    w2 = g_val * v2 / den
    first_lower = i1 < i2
    e_lo = jnp.where(first_lower, i1, i2)
    e_hi = jnp.where(first_lower, i2, i1)
    w_lo = jnp.where(first_lower, w1, w2)
    w_hi = jnp.where(first_lower, w2, w1)
    pair = ((e_lo * (2 * EXPERTS_PER_GROUP - 1 - e_lo)) >> 1) + (e_hi - e_lo - 1)
    bucket = g_idx * N_PAIRS + pair
    bucket_ref[0] = bucket

    rid_full = lax.broadcasted_iota(jnp.int32, (LANES, tm), 0)
    w_rows = jnp.where(rid_full == 0, w_lo, jnp.where(rid_full == 1, w_hi, 0.0))
    hext_ref[:, D_MODEL:] = w_rows.T

    onehot = (rid_full == bucket).astype(F32)
    trow = lax.broadcasted_iota(jnp.int32, (tm, tm), 0)
    tcol = lax.broadcasted_iota(jnp.int32, (tm, tm), 1)
    earlier = (trow < tcol).astype(BF16)
    before = jnp.dot(onehot.astype(BF16), earlier, preferred_element_type=F32) + carry_ref[...]
    rank_ref[0] = jnp.sum(onehot * before, axis=0, keepdims=True).astype(jnp.int32)
    carry_ref[...] += jnp.sum(onehot, axis=1, keepdims=True)
    counts_ref[...] = carry_ref[...]


def _route(o_sb, o_mla, x2d, g_sb, g_mla, w_out, fn, wr_hi, wr_lo, br):
    T = x2d.shape[0]
    tm = ROW_TILE
    nt = T // tm
    row = lambda n: pl.BlockSpec((tm, n), lambda i: (i, 0))
    full = lambda a: pl.BlockSpec(a.shape, lambda i: (0,) * a.ndim)
    tok = pl.BlockSpec((1, 1, tm), lambda i: (i, 0, 0))
    ins = [o_sb, o_mla, x2d, g_sb, g_mla, w_out, fn, wr_hi, wr_lo, br]
    return pl.pallas_call(
        _route_kernel,
        grid=(nt,),
        in_specs=[row(SB_WIDTH), row(MLA_WIDTH), row(D_MODEL)] + [full(a) for a in ins[3:]],
        out_specs=[row(D_MODEL), row(EXT_WIDTH), tok, tok, pl.BlockSpec((LANES, 1), lambda i: (0, 0))],
        out_shape=[
            jax.ShapeDtypeStruct((T, D_MODEL), F32),
            jax.ShapeDtypeStruct((T, EXT_WIDTH), F32),
            jax.ShapeDtypeStruct((nt, 1, tm), jnp.int32),
            jax.ShapeDtypeStruct((nt, 1, tm), jnp.int32),
            jax.ShapeDtypeStruct((LANES, 1), F32),
        ],
        scratch_shapes=[pltpu.VMEM((LANES, 1), F32)],
        compiler_params=_cparams("arbitrary"),
        name="route",
    )(*ins)


def _dest_kernel(bucket_ref, rank_ref, offs_ref, dest_ref):
    tm = bucket_ref.shape[-1]
    rid = lax.broadcasted_iota(jnp.int32, (LANES, tm), 0)
    start = jnp.sum(jnp.where(rid == bucket_ref[0], offs_ref[...], 0), axis=0, keepdims=True)
    dest_ref[0] = start + rank_ref[0]


def _dest_rows(bucket, rank, offsets):
    nt, _, tm = bucket.shape
    tok = pl.BlockSpec((1, 1, tm), lambda i: (i, 0, 0))
    return pl.pallas_call(
        _dest_kernel,
        grid=(nt,),
        in_specs=[tok, tok, pl.BlockSpec((LANES, 1), lambda i: (0, 0))],
        out_specs=tok,
        out_shape=jax.ShapeDtypeStruct((nt, 1, tm), jnp.int32),
        compiler_params=_cparams("parallel"),
        name="dest",
    )(bucket, rank, offsets)


def _scatter_kernel(dest_ref, h_ref, init_ref, xs_ref, sem):
    del init_ref
    tm = h_ref.shape[0]
    base = pl.program_id(0) * tm

    def issue(r, _):
        d = dest_ref[base + r]
        pltpu.make_async_copy(h_ref.at[pl.ds(r, 1)], xs_ref.at[pl.ds(d, 1)], sem).start()
        return _

    lax.fori_loop(0, tm, issue, 0, unroll=8)
    pltpu.make_async_copy(h_ref, xs_ref.at[pl.ds(0, tm)], sem).wait()


def _scatter_rows(dest, hext, n_rows):
    T, W = hext.shape
    tm = ROW_TILE
    init = jnp.zeros((n_rows, W), hext.dtype)
    return pl.pallas_call(
        _scatter_kernel,
        grid_spec=pltpu.PrefetchScalarGridSpec(
            num_scalar_prefetch=1,
            grid=(T // tm,),
            in_specs=[pl.BlockSpec((tm, W), lambda i, d: (i, 0)), pl.BlockSpec(memory_space=pl.ANY)],
            out_specs=pl.BlockSpec(memory_space=pl.ANY),
            scratch_shapes=[pltpu.SemaphoreType.DMA],
        ),
        out_shape=jax.ShapeDtypeStruct((n_rows, W), hext.dtype),
        input_output_aliases={2: 0},
        compiler_params=_cparams("arbitrary"),
        name="scatter",
    )(dest, hext, init)


def _moe_kernel(elo_ref, ehi_ref, nt_ref, xs_ref, wgu_lo_ref, wd_lo_ref, wgu_hi_ref, wd_hi_ref, ys_ref):
    del elo_ref, ehi_ref
    used = pl.program_id(0) < nt_ref[0]

    @pl.when(jnp.logical_not(used))
    def _():
        ys_ref[...] = jnp.zeros_like(ys_ref)

    @pl.when(used)
    def _():
        h = xs_ref[:, :D_MODEL].astype(BF16)
        gates = xs_ref[:, D_MODEL:]

        def expert(wgu_ref, wd_ref):
            gu = jnp.dot(h, wgu_ref[0], preferred_element_type=F32)
            g, u = gu[:, :D_EXPERT], gu[:, D_EXPERT:]
            hid = (g * jax.nn.sigmoid(g)) * u
            return jnp.dot(hid.astype(BF16), wd_ref[0], preferred_element_type=F32)

        ys_ref[...] = (expert(wgu_lo_ref, wd_lo_ref) * gates[:, 0:1]
                       + expert(wgu_hi_ref, wd_hi_ref) * gates[:, 1:2])


def _moe(tile_elo, tile_ehi, n_tiles_used, xs, w_gu, w_d):
    n_rows = xs.shape[0]
    tile = MOE_TILE
    rows = lambda w: pl.BlockSpec((tile, w), lambda i, elo, ehi, nt: (jnp.minimum(i, nt[0] - 1), 0))
    wspec = lambda shape, which: pl.BlockSpec(
        (1,) + shape, lambda i, elo, ehi, nt: ((elo, ehi)[which][i], 0, 0))
    return pl.pallas_call(
        _moe_kernel,
        grid_spec=pltpu.PrefetchScalarGridSpec(
            num_scalar_prefetch=3,
            grid=(n_rows // tile,),
            in_specs=[
                rows(EXT_WIDTH),
                wspec((D_MODEL, 2 * D_EXPERT), 0), wspec((D_EXPERT, D_MODEL), 0),
                wspec((D_MODEL, 2 * D_EXPERT), 1), wspec((D_EXPERT, D_MODEL), 1),
            ],
            out_specs=pl.BlockSpec((tile, D_MODEL), lambda i, elo, ehi, nt: (i, 0)),
        ),
        out_shape=jax.ShapeDtypeStruct((n_rows, D_MODEL), F32),
        compiler_params=_cparams("arbitrary"),
        name="moe",
    )(tile_elo, tile_ehi, n_tiles_used, xs, w_gu, w_d, w_gu, w_d)


def _final_kernel(dest_ref, ys_ref, x1_ref, fn_ref, o_ref, buf_ref, sem):
    tm = x1_ref.shape[0]
    base = pl.program_id(0) * tm

    def issue(r, _):
        d = dest_ref[base + r]
        pltpu.make_async_copy(ys_ref.at[pl.ds(d, 1)], buf_ref.at[pl.ds(r, 1)], sem).start()
        return _

    lax.fori_loop(0, tm, issue, 0, unroll=8)
    pltpu.make_async_copy(ys_ref.at[pl.ds(0, tm)], buf_ref, sem).wait()
    o_ref[...] = _rms(x1_ref[...] + buf_ref[...]) * fn_ref[...]


def _final(dest, ys, x1, fn):
    T, D = x1.shape
    tm = ROW_TILE
    return pl.pallas_call(
        _final_kernel,
        grid_spec=pltpu.PrefetchScalarGridSpec(
            num_scalar_prefetch=1,
            grid=(T // tm,),
            in_specs=[
                pl.BlockSpec(memory_space=pl.ANY),
                pl.BlockSpec((tm, D), lambda i, d: (i, 0)),
                pl.BlockSpec((1, D), lambda i, d: (0, 0)),
            ],
            out_specs=pl.BlockSpec((tm, D), lambda i, d: (i, 0)),
            scratch_shapes=[pltpu.VMEM((tm, D), F32), pltpu.SemaphoreType.DMA],
        ),
        out_shape=jax.ShapeDtypeStruct((T, D), F32),
        compiler_params=_cparams("arbitrary"),
        name="final",
    )(dest, ys, x1, fn)


def _pair_tables():
    lo, hi = [], []
    for g in range(N_GROUPS):
        for a in range(EXPERTS_PER_GROUP):
            for b in range(a + 1, EXPERTS_PER_GROUP):
                lo.append(g * EXPERTS_PER_GROUP + a)
                hi.append(g * EXPERTS_PER_GROUP + b)
    return np.asarray(lo, np.int32), np.asarray(hi, np.int32)


def _attention_weights(w_in, q_norm, w_uq, kv_norm, w_ukv):
    D = w_in.shape[0]
    c0 = 3 * SB_WIDTH
    zeros = lambda r, c: jnp.zeros((r, c), F32)
    w_kr = jnp.concatenate(
        [zeros(D, MLA_NOPE), w_in[:, c0 + Q_LORA + KV_LORA:], zeros(D, LANES - MLA_NOPE - MLA_ROPE)], axis=1)
    dq = MLA_NOPE + MLA_ROPE
    uq = jnp.concatenate(
        [jnp.concatenate([w_uq[:, h * dq:(h + 1) * dq], zeros(Q_LORA, LANES - dq)], axis=1)
         for h in range(MLA_HEADS)], axis=1)
    dkv = MLA_NOPE + MLA_V
    uk = jnp.concatenate(
        [jnp.concatenate([w_ukv[:, h * dkv:h * dkv + MLA_NOPE], zeros(KV_LORA, LANES - MLA_NOPE)], axis=1)
         for h in range(MLA_HEADS)], axis=1)
    uv = jnp.concatenate([w_ukv[:, h * dkv + MLA_NOPE:(h + 1) * dkv] for h in range(MLA_HEADS)], axis=1)
    return {
        "sb": w_in[:, :c0].astype(BF16),
        "cq": w_in[:, c0:c0 + Q_LORA].astype(BF16),
        "ckv": w_in[:, c0 + Q_LORA:c0 + Q_LORA + KV_LORA].astype(BF16),
        "kr": w_kr.astype(BF16),
        "qn": q_norm[None, :],
        "uq": uq.astype(BF16),
        "kvn": kv_norm[None, :],
        "uk": uk.astype(BF16),
        "uv": uv.astype(BF16),
    }


def _router_weights(w_group, b_group, w_expert, b_expert):
    D = w_group.shape[0]
    pad_g = SUBLANES - N_GROUPS
    pad_e = LANES - SUBLANES - N_EXPERTS
    w = jnp.concatenate([w_group, jnp.zeros((D, pad_g), F32), w_expert, jnp.zeros((D, pad_e), F32)], axis=1)
    b = jnp.concatenate([b_group, jnp.zeros((pad_g,), F32), b_expert, jnp.zeros((pad_e,), F32)])[None, :]
    w_hi, w_lo = _split_bf16(w)
    return w_hi, w_lo, b


def _bucket_layout(counts, n_tiles):
    c = counts[:N_BUCKETS, 0].astype(jnp.int32)
    tiles = (c + MOE_TILE - 1) // MOE_TILE
    tile_end = jnp.cumsum(tiles)
    offsets = (tile_end - tiles) * MOE_TILE
    offsets = jnp.concatenate([offsets, jnp.zeros((LANES - N_BUCKETS,), jnp.int32)])[:, None]
    n_used = tile_end[-1]
    tile_id = jnp.minimum(jnp.arange(n_tiles, dtype=jnp.int32), n_used - 1)
    tile_bucket = jnp.searchsorted(tile_end, tile_id, side="right").astype(jnp.int32)
    pair_lo, pair_hi = _pair_tables()
    return offsets, jnp.asarray(pair_lo)[tile_bucket], jnp.asarray(pair_hi)[tile_bucket], n_used[None]


def kernel(x, positions, attn_norm, w_in, q_norm, w_uq, kv_norm, w_ukv, sb_out_norm, mla_out_norm, w_out,
           ffn_norm, w_group_router, b_group_router, w_expert_router, b_expert_router, w_gate, w_up, w_down,
           final_norm):
    B, S, D = x.shape
    T = B * S
    depth = w_in.shape[0]
    assert D == D_MODEL and T % ROW_TILE == 0 and S % ATT_TILE == 0
    assert depth == 1, "the final norm is fused into the last layer's gather kernel"
    n_sorted_tiles = T // MOE_TILE + N_BUCKETS
    n_sorted_rows = n_sorted_tiles * MOE_TILE

    lane = jnp.arange(LANES)
    invf = (ROPE_BASE ** (-(lane % ROPE_HALF).astype(F32) / ROPE_HALF))[None, :]
    pos2d = positions.reshape(T, 1)
    x2d = x.reshape(T, D)
    for l in range(depth):
        aw = _attention_weights(w_in[l], q_norm[l], w_uq[l], kv_norm[l], w_ukv[l])
        q_sb, k_sb, v_sb, q_m, k_m, v_m = _projections(x2d, pos2d, invf, attn_norm[l][None, :], aw)
        o_sb = _sb_attention(q_sb, k_sb, v_sb, B, S)
        o_mla = _mla_attention(q_m, k_m, v_m, B, S)

        wr_hi, wr_lo, br = _router_weights(w_group_router[l], b_group_router[l], w_expert_router[l], b_expert_router[l])
        x1, hext, bucket, rank, counts = _route(
            o_sb, o_mla, x2d, sb_out_norm[l][None, :], mla_out_norm[l][None, :], w_out[l].astype(BF16),
            ffn_norm[l][None, :], wr_hi, wr_lo, br)
        offsets, tile_elo, tile_ehi, n_used = _bucket_layout(counts, n_sorted_tiles)
        dest = _dest_rows(bucket, rank, offsets).reshape(T)

        xs = _scatter_rows(dest, hext, n_sorted_rows)
        w_gu = jnp.concatenate([w_gate[l], w_up[l]], axis=-1).astype(BF16)
        ys = _moe(tile_elo, tile_ehi, n_used, xs, w_gu, w_down[l].astype(BF16))
        x2d = _final(dest, ys, x1, final_norm[None, :])
    return x2d.reshape(B, S, D)
```

```python
import functools

import jax
import jax.numpy as jnp
import numpy as np
from jax import lax
from jax.experimental import pallas as pl
from jax.experimental.pallas import tpu as pltpu

F32 = jnp.float32
BF16 = jnp.bfloat16

D_MODEL = 1024
SB_HEADS = 8
SB_HEAD_DIM = 64
SB_WIDTH = SB_HEADS * SB_HEAD_DIM
MLA_HEADS = 8
MLA_NOPE = 64
MLA_ROPE = 32
MLA_V = 64
MLA_WIDTH = MLA_HEADS * MLA_V
Q_LORA = 256
KV_LORA = 128
ROPE_BASE = 10000.0
N_GROUPS = 4
EXPERTS_PER_GROUP = 8
N_EXPERTS = N_GROUPS * EXPERTS_PER_GROUP
D_EXPERT = 256
EPS = 1e-6
LOG2_E = 1.4426950408889634

LANES = 128
SUBLANES = 8
N_PAIRS = EXPERTS_PER_GROUP * (EXPERTS_PER_GROUP - 1) // 2
N_BUCKETS = N_GROUPS * N_PAIRS
assert N_BUCKETS <= LANES
ROPE_HALF = MLA_ROPE // 2
HEAD_PAIRS = SB_HEADS // 2
assert SB_HEADS == MLA_HEADS and 2 * SB_HEAD_DIM == LANES and 2 * MLA_V == LANES

ROW_TILE = 512
ATT_TILE = 256
ATT_CHUNK = LANES
MOE_TILE = 256
EXT_WIDTH = D_MODEL + LANES
SB_UNDERFLOW = 151.0
VMEM_LIMIT = 56 * 1024 * 1024


def _rms(x):
    return x * lax.rsqrt(jnp.mean(x * x, axis=-1, keepdims=True) + EPS)


def _cparams(*sem):
    return pltpu.CompilerParams(dimension_semantics=sem, vmem_limit_bytes=VMEM_LIMIT)


def _proj_kernel(x_ref, pos_ref, invf_ref, an_ref, wsb_ref, wvsb_ref, wcq_ref, wckv_ref, wkr_ref, qn_ref, wuq_ref,
                 kvn_ref, wuk_ref, wuv_ref, qsb_ref, ksb_ref, vsb_ref, qm_ref, km_ref, vm_ref):
    tk = ATT_TILE
    nt_dims = (((1,), (1,)), ((), ()))
    hb = (_rms(x_ref[...]) * an_ref[...]).astype(BF16)
    sb = jnp.dot(hb, wsb_ref[...], preferred_element_type=F32)
    qsb_ref[...] = (sb[:, :SB_WIDTH] * (SB_HEAD_DIM ** -0.5 * LOG2_E)).astype(BF16)
    ksb_ref[...] = sb[:, SB_WIDTH:].astype(BF16)
    v_sb = lax.dot_general(wvsb_ref[...], hb, nt_dims, preferred_element_type=F32).astype(BF16)
    for kb in range(v_sb.shape[1] // tk):
        vsb_ref[kb] = v_sb[:, kb * tk:(kb + 1) * tk]

    cq = jnp.dot(hb, wcq_ref[...], preferred_element_type=F32)
    ckv = jnp.dot(hb, wckv_ref[...], preferred_element_type=F32)
    kr = jnp.dot(hb, wkr_ref[...], preferred_element_type=F32)
    q = jnp.dot((_rms(cq) * qn_ref[...]).astype(BF16), wuq_ref[...], preferred_element_type=F32)
    ckn = (_rms(ckv) * kvn_ref[...]).astype(BF16)
    kn = jnp.dot(ckn, wuk_ref[...], preferred_element_type=F32)
    v_m = lax.dot_general(wuv_ref[...], ckn, nt_dims, preferred_element_type=F32)
    ones_rows = (lax.broadcasted_iota(jnp.int32, (MLA_HEADS * LANES, 1), 0) % LANES) >= MLA_V
    v_m = jnp.where(ones_rows, 1.0, v_m).astype(BF16)
    for kb in range(v_m.shape[1] // tk):
        vm_ref[kb] = v_m[:, kb * tk:(kb + 1) * tk]

    ang = pos_ref[...].astype(F32) * invf_ref[...]
    cos, sin = jnp.cos(ang), jnp.sin(ang)
    lane = lax.broadcasted_iota(jnp.int32, (1, LANES), 1)
    x1_lanes = (lane >= MLA_NOPE) & (lane < MLA_NOPE + ROPE_HALF)
    x2_lanes = (lane >= MLA_NOPE + ROPE_HALF) & (lane < MLA_NOPE + MLA_ROPE)
    c_tab = jnp.where(lane < MLA_NOPE, 1.0, jnp.where(x1_lanes | x2_lanes, cos, 0.0))
    s_from_x2 = jnp.where(x1_lanes, -sin, 0.0)
    s_from_x1 = jnp.where(x2_lanes, sin, 0.0)

    def rope(t):
        return (t * c_tab + pltpu.roll(t, LANES - ROPE_HALF, 1) * s_from_x2
                + pltpu.roll(t, ROPE_HALF, 1) * s_from_x1)

    k_rope = rope(kr)
    q_scale = (MLA_NOPE + MLA_ROPE) ** -0.5 * LOG2_E
    for h in range(MLA_HEADS):
        blk = slice(h * LANES, (h + 1) * LANES)
        qm_ref[:, blk] = (rope(q[:, blk]) * q_scale).astype(BF16)
        km_ref[:, blk] = (kn[:, blk] + k_rope).astype(BF16)


def _projections(x2d, pos2d, invf, an, w):
    T = x2d.shape[0]
    tm = ROW_TILE
    tk = ATT_TILE
    row = lambda n: pl.BlockSpec((tm, n), lambda i: (i, 0))
    slab = lambda n: pl.BlockSpec((tm // tk, n, tk), lambda i: (i, 0, 0))
    full = lambda a: pl.BlockSpec(a.shape, lambda i: (0,) * a.ndim)
    ins = [x2d, pos2d, invf, an, w["sb"], w["vsb"], w["cq"], w["ckv"], w["kr"], w["qn"], w["uq"], w["kvn"],
           w["uk"], w["uv"]]
    in_specs = [row(D_MODEL), row(1)] + [full(a) for a in ins[2:]]
    wide = MLA_HEADS * LANES
    row_out = lambda n: jax.ShapeDtypeStruct((T, n), BF16)
    slab_out = lambda n: jax.ShapeDtypeStruct((T // tk, n, tk), BF16)
    return pl.pallas_call(
        _proj_kernel,
        grid=(T // tm,),
        in_specs=in_specs,
        out_specs=[row(SB_WIDTH), row(SB_WIDTH), slab(SB_WIDTH), row(wide), row(wide), slab(wide)],
        out_shape=[row_out(SB_WIDTH), row_out(SB_WIDTH), slab_out(SB_WIDTH), row_out(wide), row_out(wide),
                   slab_out(wide)],
        compiler_params=_cparams("parallel"),
        name="proj",
    )(*ins)


def _softplus2(z):
    return jnp.maximum(z, 0.0) + jnp.log2(1.0 + jnp.exp2(-jnp.abs(z)))


def _sb_kernel(q_ref, k_ref, v_ref, o_ref, z_ref, w_ref, acc_ref, c_ref):
    tq = tk = ATT_TILE
    cw = ATT_CHUNK
    i = pl.program_id(1)
    lane = lax.broadcasted_iota(jnp.int32, (1, LANES), 1)
    in_head = (lane < SB_HEAD_DIM, lane >= SB_HEAD_DIM)
    later = (lax.broadcasted_iota(jnp.int32, (tk, tk), 1)
             > lax.broadcasted_iota(jnp.int32, (tk, tk), 0)).astype(BF16)
    key_idx = lax.broadcasted_iota(jnp.int32, (tk, cw), 0)
    qry_idx = lax.broadcasted_iota(jnp.int32, (tk, cw), 1)
    strict = [key_idx < qry_idx + c * cw for c in range(tq // cw)]
    nt_dims = (((1,), (1,)), ((), ()))
    pair_lanes = lambda h: slice((h // 2) * LANES, (h // 2 + 1) * LANES)

    def visit(j, diagonal):
        keys = pl.ds(pl.multiple_of(j * tk, tk), tk)
        for h in range(SB_HEADS):
            q_pair = q_ref[:, pair_lanes(h)]
            qh = jnp.where(in_head[h % 2], q_pair, jnp.zeros_like(q_pair))
            z_ref[h] = lax.dot_general(k_ref[keys, pair_lanes(h)], qh, nt_dims, preferred_element_type=F32)
        c_low = None
        for h in range(SB_HEADS):
            for c in range(tq // cw):
                cols = slice(c * cw, (c + 1) * cw)
                nk = min(tk, (c + 1) * cw) if diagonal else tk
                z = z_ref[h, :nk, cols]
                sp = _softplus2(z)
                if diagonal:
                    sp = jnp.where(strict[c][:nk], sp, 0.0)
                    c_new = jnp.sum(sp, axis=0, keepdims=True)
                    z_ref[h, :nk, cols] = z - sp
                    if nk < tk:
                        w_ref[h, nk:, cols] = jnp.zeros((tk - nk, cw), BF16)
                else:
                    c_old = c_ref[h, :, cols]
                    c_new = c_old + jnp.sum(sp, axis=0, keepdims=True)
                    z_ref[h, :, cols] = (z - sp) - c_old
                w_ref[h, :nk, cols] = sp.astype(BF16)
                c_ref[h, :, cols] = c_new
                c_low = c_new if c_low is None else jnp.minimum(c_low, c_new)
        for h in range(SB_HEADS):
            suffix = jnp.dot(later, w_ref[h], preferred_element_type=F32)
            for c in range(tq // cw):
                cols = slice(c * cw, (c + 1) * cw)
                nk = min(tk, (c + 1) * cw) if diagonal else tk
                a = jnp.exp2(z_ref[h, :nk, cols] - suffix[:nk, cols])
                if diagonal:
                    a = jnp.where(strict[c][:nk], a, 0.0)
                w_ref[h, :nk, cols] = a.astype(BF16)
        for h in range(SB_HEADS):
            v_head = v_ref[j, h * SB_HEAD_DIM:(h + 1) * SB_HEAD_DIM, :]
            av = jnp.dot(v_head, w_ref[h], preferred_element_type=F32)
            if diagonal:
                acc_ref[h] = av
            else:
                acc_ref[h] += av
        return jnp.min(c_low)

    def cond(carry):
        j, c_min = carry
        return (j >= 0) & (c_min < SB_UNDERFLOW)

    def body(carry):
        j, _ = carry
        return j - 1, visit(j, diagonal=False)

    lax.while_loop(cond, body, (i - 1, visit(i, diagonal=True)))
    for p in range(HEAD_PAIRS):
        o_ref[:, p * LANES:(p + 1) * LANES] = jnp.concatenate([acc_ref[2 * p], acc_ref[2 * p + 1]], axis=0).T


def _sb_attention(q, k, v, B, S):
    tq = tk = ATT_TILE
    nq = S // tq
    return pl.pallas_call(
        _sb_kernel,
        grid=(B, nq),
        in_specs=[
            pl.BlockSpec((tq, SB_WIDTH), lambda b, i: (b * nq + i, 0)),
            pl.BlockSpec((S, SB_WIDTH), lambda b, i: (b, 0)),
            pl.BlockSpec((S // tk, SB_WIDTH, tk), lambda b, i: (b, 0, 0)),
        ],
        out_specs=pl.BlockSpec((tq, SB_WIDTH), lambda b, i: (b * nq + i, 0)),
        out_shape=jax.ShapeDtypeStruct((B * S, SB_WIDTH), F32),
        scratch_shapes=[
            pltpu.VMEM((SB_HEADS, tk, tq), F32),
            pltpu.VMEM((SB_HEADS, tk, tq), BF16),
            pltpu.VMEM((SB_HEADS, SB_HEAD_DIM, tq), F32),
            pltpu.VMEM((SB_HEADS, 1, tq), F32),
        ],
        compiler_params=_cparams("parallel", "parallel"),
        name="sb_attn",
    )(q, k, v)


def _mla_kernel(q_ref, k_ref, v_ref, o_ref, s_ref, p_ref, acc_ref, m_ref, alpha_ref):
    tq = tk = ATT_TILE
    cw = ATT_CHUNK
    i = pl.program_id(1)
    key_idx = lax.broadcasted_iota(jnp.int32, (tk, cw), 0)
    qry_idx = lax.broadcasted_iota(jnp.int32, (tk, cw), 1)
    causal = [key_idx <= qry_idx + c * cw for c in range(tq // cw)]
    nt_dims = (((1,), (1,)), ((), ()))
    head_lanes = lambda h: slice(h * LANES, (h + 1) * LANES)

    def visit(j, diagonal):
        keys = pl.ds(pl.multiple_of(j * tk, tk), tk)
        for h in range(MLA_HEADS):
            s_ref[h] = lax.dot_general(k_ref[keys, head_lanes(h)], q_ref[:, head_lanes(h)], nt_dims,
                                       preferred_element_type=F32)
        for h in range(MLA_HEADS):
            for c in range(tq // cw):
                cols = slice(c * cw, (c + 1) * cw)
                nk = min(tk, (c + 1) * cw) if diagonal else tk
                s = s_ref[h, :nk, cols]
                if diagonal:
                    s = jnp.where(causal[c][:nk], s, -jnp.inf)
                    m_new = jnp.max(s, axis=0, keepdims=True)
                    if nk < tk:
                        p_ref[h, nk:, cols] = jnp.zeros((tk - nk, cw), BF16)
                else:
                    m_old = m_ref[h, :, cols]
                    m_new = jnp.maximum(m_old, jnp.max(s, axis=0, keepdims=True))
                    alpha_ref[h, :, cols] = jnp.exp2(m_old - m_new)
                p_ref[h, :nk, cols] = jnp.exp2(s - m_new).astype(BF16)
                m_ref[h, :, cols] = m_new
        for h in range(MLA_HEADS):
            pv = jnp.dot(v_ref[j, head_lanes(h), :], p_ref[h], preferred_element_type=F32)
            if diagonal:
                acc_ref[h] = pv
            else:
                acc_ref[h] = alpha_ref[h] * acc_ref[h] + pv

    visit(i, diagonal=True)

    def body(j, carry):
        visit(j, diagonal=False)
        return carry

    lax.fori_loop(0, i, body, 0)
    for p in range(HEAD_PAIRS):
        outs = [acc_ref[2 * p + e, :MLA_V] / acc_ref[2 * p + e, MLA_V:] for e in range(2)]
        o_ref[:, p * LANES:(p + 1) * LANES] = jnp.concatenate(outs, axis=0).T


def _mla_attention(q, k, v, B, S):
    tq = tk = ATT_TILE
    nq = S // tq
    width = MLA_HEADS * LANES
    return pl.pallas_call(
        _mla_kernel,
        grid=(B, nq),
        in_specs=[
            pl.BlockSpec((tq, width), lambda b, i: (b * nq + i, 0)),
            pl.BlockSpec((S, width), lambda b, i: (b, 0)),
            pl.BlockSpec((S // tk, width, tk), lambda b, i: (b, 0, 0)),
        ],
        out_specs=pl.BlockSpec((tq, MLA_WIDTH), lambda b, i: (b * nq + i, 0)),
        out_shape=jax.ShapeDtypeStruct((B * S, MLA_WIDTH), F32),
        scratch_shapes=[
            pltpu.VMEM((MLA_HEADS, tk, tq), F32),
            pltpu.VMEM((MLA_HEADS, tk, tq), BF16),
            pltpu.VMEM((MLA_HEADS, LANES, tq), F32),
            pltpu.VMEM((MLA_HEADS, 1, tq), F32),
            pltpu.VMEM((MLA_HEADS, 1, tq), F32),
        ],
        compiler_params=_cparams("parallel", "parallel"),
        name="mla_attn",
    )(q, k, v)


ROUTER_ROWS = SUBLANES * (1 + N_GROUPS)


def _split_bf16(a):
    hi = a.astype(BF16)
    return hi, (a - hi.astype(F32)).astype(BF16)


def _route_kernel(osb_ref, omla_ref, x_ref, g_sb_ref, g_mla_ref, wout_ref, fn_ref, wr_hi_ref, wr_lo_ref, br_ref,
                  x1_ref, hext_ref, bucket_ref, rank_ref, counts_ref, carry_ref):
    tm = ROW_TILE

    @pl.when(pl.program_id(0) == 0)
    def _():
        carry_ref[...] = jnp.zeros_like(carry_ref)

    o = jnp.concatenate([_rms(osb_ref[...]) * g_sb_ref[...], _rms(omla_ref[...]) * g_mla_ref[...]], axis=-1)
    x1 = x_ref[...] + jnp.dot(o.astype(BF16), wout_ref[...], preferred_element_type=F32)
    x1_ref[...] = x1
    h = _rms(x1) * fn_ref[...]
    hext_ref[:, :D_MODEL] = h

    h_hi, h_lo = _split_bf16(h)
    logits = (jnp.dot(h_hi, wr_hi_ref[...], preferred_element_type=F32)
              + jnp.dot(h_hi, wr_lo_ref[...], preferred_element_type=F32)
              + jnp.dot(h_lo, wr_hi_ref[...], preferred_element_type=F32)) + br_ref[...]
    lt = logits.T

    rid = lax.broadcasted_iota(jnp.int32, (SUBLANES, tm), 0)
    g_logit = jnp.where(rid < N_GROUPS, lt[:SUBLANES], -jnp.inf)
    g_exp = jnp.exp(g_logit - jnp.max(g_logit, axis=0, keepdims=True))
    p_group = g_exp / jnp.sum(g_exp, axis=0, keepdims=True)
    g_val = jnp.max(p_group, axis=0, keepdims=True)
    g_idx = jnp.min(jnp.where(p_group == g_val, rid, SUBLANES), axis=0, keepdims=True)
    local = lt[SUBLANES * N_GROUPS:SUBLANES * (N_GROUPS + 1)]
    for g in range(N_GROUPS - 2, -1, -1):
        local = jnp.where(g_idx == g, lt[SUBLANES * (g + 1):SUBLANES * (g + 2)], local)
    e_exp = jnp.exp(local - jnp.max(local, axis=0, keepdims=True))
    p_exp = e_exp / jnp.sum(e_exp, axis=0, keepdims=True)
    v1 = jnp.max(p_exp, axis=0, keepdims=True)
    i1 = jnp.min(jnp.where(p_exp == v1, rid, SUBLANES), axis=0, keepdims=True)
    rest = jnp.where(rid == i1, -1.0, p_exp)
    v2 = jnp.max(rest, axis=0, keepdims=True)
    i2 = jnp.min(jnp.where(rest == v2, rid, SUBLANES), axis=0, keepdims=True)
    den = v1 + v2
    w1 = g_val * v1 / den
    w2 = g_val * v2 / den
    first_lower = i1 < i2
    e_lo = jnp.where(first_lower, i1, i2)
    e_hi = jnp.where(first_lower, i2, i1)
    w_lo = jnp.where(first_lower, w1, w2)
    w_hi = jnp.where(first_lower, w2, w1)
    pair = ((e_lo * (2 * EXPERTS_PER_GROUP - 1 - e_lo)) >> 1) + (e_hi - e_lo - 1)
    bucket = g_idx * N_PAIRS + pair
    bucket_ref[0] = bucket

    rid_full = lax.broadcasted_iota(jnp.int32, (LANES, tm), 0)
    w_rows = jnp.where(rid_full == 0, w_lo, jnp.where(rid_full == 1, w_hi, 0.0))
    hext_ref[:, D_MODEL:] = w_rows.T

    onehot = (rid_full == bucket).astype(F32)
    trow = lax.broadcasted_iota(jnp.int32, (tm, tm), 0)
    tcol = lax.broadcasted_iota(jnp.int32, (tm, tm), 1)
    earlier = (trow < tcol).astype(BF16)
    before = jnp.dot(onehot.astype(BF16), earlier, preferred_element_type=F32) + carry_ref[...]
    rank_ref[0] = jnp.sum(onehot * before, axis=0, keepdims=True).astype(jnp.int32)
    carry_ref[...] += jnp.sum(onehot, axis=1, keepdims=True)
    counts_ref[...] = carry_ref[...]


def _route(o_sb, o_mla, x2d, g_sb, g_mla, w_out, fn, wr_hi, wr_lo, br):
    T = x2d.shape[0]
    tm = ROW_TILE
    nt = T // tm
    row = lambda n: pl.BlockSpec((tm, n), lambda i: (i, 0))
    full = lambda a: pl.BlockSpec(a.shape, lambda i: (0,) * a.ndim)
    tok = pl.BlockSpec((1, 1, tm), lambda i: (i, 0, 0))
    ins = [o_sb, o_mla, x2d, g_sb, g_mla, w_out, fn, wr_hi, wr_lo, br]
    return pl.pallas_call(
        _route_kernel,
        grid=(nt,),
        in_specs=[row(SB_WIDTH), row(MLA_WIDTH), row(D_MODEL)] + [full(a) for a in ins[3:]],
        out_specs=[row(D_MODEL), row(EXT_WIDTH), tok, tok, pl.BlockSpec((LANES, 1), lambda i: (0, 0))],
        out_shape=[
            jax.ShapeDtypeStruct((T, D_MODEL), F32),
            jax.ShapeDtypeStruct((T, EXT_WIDTH), F32),
            jax.ShapeDtypeStruct((nt, 1, tm), jnp.int32),
            jax.ShapeDtypeStruct((nt, 1, tm), jnp.int32),
            jax.ShapeDtypeStruct((LANES, 1), F32),
        ],
        scratch_shapes=[pltpu.VMEM((LANES, 1), F32)],
        compiler_params=_cparams("arbitrary"),
        name="route",
    )(*ins)


def _dest_kernel(bucket_ref, rank_ref, offs_ref, dest_ref):
    tm = bucket_ref.shape[-1]
    rid = lax.broadcasted_iota(jnp.int32, (LANES, tm), 0)
    start = jnp.sum(jnp.where(rid == bucket_ref[0], offs_ref[...], 0), axis=0, keepdims=True)
    dest_ref[0] = start + rank_ref[0]


def _dest_rows(bucket, rank, offsets):
    nt, _, tm = bucket.shape
    tok = pl.BlockSpec((1, 1, tm), lambda i: (i, 0, 0))
    return pl.pallas_call(
        _dest_kernel,
        grid=(nt,),
        in_specs=[tok, tok, pl.BlockSpec((LANES, 1), lambda i: (0, 0))],
        out_specs=tok,
        out_shape=jax.ShapeDtypeStruct((nt, 1, tm), jnp.int32),
        compiler_params=_cparams("parallel"),
        name="dest",
    )(bucket, rank, offsets)


def _scatter_kernel(dest_ref, h_ref, init_ref, xs_ref, sem):
    del init_ref
    tm = h_ref.shape[0]
    base = pl.program_id(0) * tm

    def issue(r, _):
        d = dest_ref[base + r]
        pltpu.make_async_copy(h_ref.at[pl.ds(r, 1)], xs_ref.at[pl.ds(d, 1)], sem).start()
        return _

    lax.fori_loop(0, tm, issue, 0, unroll=8)
    pltpu.make_async_copy(h_ref, xs_ref.at[pl.ds(0, tm)], sem).wait()


def _scatter_rows(dest, hext, n_rows):
    T, W = hext.shape
    tm = ROW_TILE
    init = jnp.zeros((n_rows, W), hext.dtype)
    return pl.pallas_call(
        _scatter_kernel,
        grid_spec=pltpu.PrefetchScalarGridSpec(
            num_scalar_prefetch=1,
            grid=(T // tm,),
            in_specs=[pl.BlockSpec((tm, W), lambda i, d: (i, 0)), pl.BlockSpec(memory_space=pl.ANY)],
            out_specs=pl.BlockSpec(memory_space=pl.ANY),
            scratch_shapes=[pltpu.SemaphoreType.DMA],
        ),
        out_shape=jax.ShapeDtypeStruct((n_rows, W), hext.dtype),
        input_output_aliases={2: 0},
        compiler_params=_cparams("arbitrary"),
        name="scatter",
    )(dest, hext, init)


def _moe_kernel(elo_ref, ehi_ref, nt_ref, xs_ref, wgu_lo_ref, wd_lo_ref, wgu_hi_ref, wd_hi_ref, ys_ref):
    del elo_ref, ehi_ref
    used = pl.program_id(0) < nt_ref[0]

    @pl.when(jnp.logical_not(used))
    def _():
        ys_ref[...] = jnp.zeros_like(ys_ref)

    @pl.when(used)
    def _():
        h = xs_ref[:, :D_MODEL].astype(BF16)
        gates = xs_ref[:, D_MODEL:]

        def expert(wgu_ref, wd_ref):
            gu = jnp.dot(h, wgu_ref[0], preferred_element_type=F32)
            g, u = gu[:, :D_EXPERT], gu[:, D_EXPERT:]
            hid = (g * jax.nn.sigmoid(g)) * u
            return jnp.dot(hid.astype(BF16), wd_ref[0], preferred_element_type=F32)

        ys_ref[...] = (expert(wgu_lo_ref, wd_lo_ref) * gates[:, 0:1]
                       + expert(wgu_hi_ref, wd_hi_ref) * gates[:, 1:2])


def _moe(tile_elo, tile_ehi, n_tiles_used, xs, w_gu, w_d):
    n_rows = xs.shape[0]
    tile = MOE_TILE
    rows = lambda w: pl.BlockSpec((tile, w), lambda i, elo, ehi, nt: (jnp.minimum(i, nt[0] - 1), 0))
    wspec = lambda shape, which: pl.BlockSpec(
        (1,) + shape, lambda i, elo, ehi, nt: ((elo, ehi)[which][i], 0, 0))
    return pl.pallas_call(
        _moe_kernel,
        grid_spec=pltpu.PrefetchScalarGridSpec(
            num_scalar_prefetch=3,
            grid=(n_rows // tile,),
            in_specs=[
                rows(EXT_WIDTH),
                wspec((D_MODEL, 2 * D_EXPERT), 0), wspec((D_EXPERT, D_MODEL), 0),
                wspec((D_MODEL, 2 * D_EXPERT), 1), wspec((D_EXPERT, D_MODEL), 1),
            ],
            out_specs=pl.BlockSpec((tile, D_MODEL), lambda i, elo, ehi, nt: (i, 0)),
        ),
        out_shape=jax.ShapeDtypeStruct((n_rows, D_MODEL), F32),
        compiler_params=_cparams("arbitrary"),
        name="moe",
    )(tile_elo, tile_ehi, n_tiles_used, xs, w_gu, w_d, w_gu, w_d)


def _final_kernel(dest_ref, ys_ref, x1_ref, fn_ref, o_ref, buf_ref, sem):
    tm = x1_ref.shape[0]
    base = pl.program_id(0) * tm

    def issue(r, _):
        d = dest_ref[base + r]
        pltpu.make_async_copy(ys_ref.at[pl.ds(d, 1)], buf_ref.at[pl.ds(r, 1)], sem).start()
        return _

    lax.fori_loop(0, tm, issue, 0, unroll=8)
    pltpu.make_async_copy(ys_ref.at[pl.ds(0, tm)], buf_ref, sem).wait()
    o_ref[...] = _rms(x1_ref[...] + buf_ref[...]) * fn_ref[...]


def _final(dest, ys, x1, fn):
    T, D = x1.shape
    tm = ROW_TILE
    return pl.pallas_call(
        _final_kernel,
        grid_spec=pltpu.PrefetchScalarGridSpec(
            num_scalar_prefetch=1,
            grid=(T // tm,),
            in_specs=[
                pl.BlockSpec(memory_space=pl.ANY),
                pl.BlockSpec((tm, D), lambda i, d: (i, 0)),
                pl.BlockSpec((1, D), lambda i, d: (0, 0)),
            ],
            out_specs=pl.BlockSpec((tm, D), lambda i, d: (i, 0)),
            scratch_shapes=[pltpu.VMEM((tm, D), F32), pltpu.SemaphoreType.DMA],
        ),
        out_shape=jax.ShapeDtypeStruct((T, D), F32),
        compiler_params=_cparams("arbitrary"),
        name="final",
    )(dest, ys, x1, fn)


def _pair_tables():
    lo, hi = [], []
    for g in range(N_GROUPS):
        for a in range(EXPERTS_PER_GROUP):
            for b in range(a + 1, EXPERTS_PER_GROUP):
                lo.append(g * EXPERTS_PER_GROUP + a)
                hi.append(g * EXPERTS_PER_GROUP + b)
    return np.asarray(lo, np.int32), np.asarray(hi, np.int32)


def _attention_weights(w_in, q_norm, w_uq, kv_norm, w_ukv):
    D = w_in.shape[0]
    c0 = 3 * SB_WIDTH
    zeros = lambda r, c: jnp.zeros((r, c), F32)
    w_kr = jnp.concatenate(
        [zeros(D, MLA_NOPE), w_in[:, c0 + Q_LORA + KV_LORA:], zeros(D, LANES - MLA_NOPE - MLA_ROPE)], axis=1)
    dq = MLA_NOPE + MLA_ROPE
    uq = jnp.concatenate(
        [jnp.concatenate([w_uq[:, h * dq:(h + 1) * dq], zeros(Q_LORA, LANES - dq)], axis=1)
         for h in range(MLA_HEADS)], axis=1)
    dkv = MLA_NOPE + MLA_V
    uk = jnp.concatenate(
        [jnp.concatenate([w_ukv[:, h * dkv:h * dkv + MLA_NOPE], zeros(KV_LORA, LANES - MLA_NOPE)], axis=1)
         for h in range(MLA_HEADS)], axis=1)
    uv = jnp.concatenate(
        [jnp.concatenate([w_ukv[:, h * dkv + MLA_NOPE:(h + 1) * dkv].T, zeros(LANES - MLA_V, KV_LORA)], axis=0)
         for h in range(MLA_HEADS)], axis=0)
    return {
        "sb": w_in[:, :2 * SB_WIDTH].astype(BF16),
        "vsb": w_in[:, 2 * SB_WIDTH:c0].T.astype(BF16),
        "cq": w_in[:, c0:c0 + Q_LORA].astype(BF16),
        "ckv": w_in[:, c0 + Q_LORA:c0 + Q_LORA + KV_LORA].astype(BF16),
        "kr": w_kr.astype(BF16),
        "qn": q_norm[None, :],
        "uq": uq.astype(BF16),
        "kvn": kv_norm[None, :],
        "uk": uk.astype(BF16),
        "uv": uv.astype(BF16),
    }


def _router_weights(w_group, b_group, w_expert, b_expert):
    D = w_group.shape[0]
    pad_g = SUBLANES - N_GROUPS
    pad_e = LANES - SUBLANES - N_EXPERTS
    w = jnp.concatenate([w_group, jnp.zeros((D, pad_g), F32), w_expert, jnp.zeros((D, pad_e), F32)], axis=1)
    b = jnp.concatenate([b_group, jnp.zeros((pad_g,), F32), b_expert, jnp.zeros((pad_e,), F32)])[None, :]
    w_hi, w_lo = _split_bf16(w)
    return w_hi, w_lo, b


def _bucket_layout(counts, n_tiles):
    c = counts[:N_BUCKETS, 0].astype(jnp.int32)
    tiles = (c + MOE_TILE - 1) // MOE_TILE
    tile_end = jnp.cumsum(tiles)
    offsets = (tile_end - tiles) * MOE_TILE
    offsets = jnp.concatenate([offsets, jnp.zeros((LANES - N_BUCKETS,), jnp.int32)])[:, None]
    n_used = tile_end[-1]
    tile_id = jnp.minimum(jnp.arange(n_tiles, dtype=jnp.int32), n_used - 1)
    tile_bucket = jnp.searchsorted(tile_end, tile_id, side="right").astype(jnp.int32)
    pair_lo, pair_hi = _pair_tables()
    return offsets, jnp.asarray(pair_lo)[tile_bucket], jnp.asarray(pair_hi)[tile_bucket], n_used[None]


def kernel(x, positions, attn_norm, w_in, q_norm, w_uq, kv_norm, w_ukv, sb_out_norm, mla_out_norm, w_out,
           ffn_norm, w_group_router, b_group_router, w_expert_router, b_expert_router, w_gate, w_up, w_down,
           final_norm):
    B, S, D = x.shape
    T = B * S
    depth = w_in.shape[0]
    assert D == D_MODEL and T % ROW_TILE == 0 and S % ATT_TILE == 0
    assert depth == 1, "the final norm is fused into the last layer's gather kernel"
    n_sorted_tiles = T // MOE_TILE + N_BUCKETS
    n_sorted_rows = n_sorted_tiles * MOE_TILE

    lane = jnp.arange(LANES)
    invf = (ROPE_BASE ** (-(lane % ROPE_HALF).astype(F32) / ROPE_HALF))[None, :]
    pos2d = positions.reshape(T, 1)
    x2d = x.reshape(T, D)
    for l in range(depth):
        aw = _attention_weights(w_in[l], q_norm[l], w_uq[l], kv_norm[l], w_ukv[l])
        q_sb, k_sb, v_sb, q_m, k_m, v_m = _projections(x2d, pos2d, invf, attn_norm[l][None, :], aw)
        o_sb = _sb_attention(q_sb, k_sb, v_sb, B, S)
        o_mla = _mla_attention(q_m, k_m, v_m, B, S)

        wr_hi, wr_lo, br = _router_weights(w_group_router[l], b_group_router[l], w_expert_router[l], b_expert_router[l])
        x1, hext, bucket, rank, counts = _route(
            o_sb, o_mla, x2d, sb_out_norm[l][None, :], mla_out_norm[l][None, :], w_out[l].astype(BF16),
            ffn_norm[l][None, :], wr_hi, wr_lo, br)
        offsets, tile_elo, tile_ehi, n_used = _bucket_layout(counts, n_sorted_tiles)
        dest = _dest_rows(bucket, rank, offsets).reshape(T)

        xs = _scatter_rows(dest, hext, n_sorted_rows)
        w_gu = jnp.concatenate([w_gate[l], w_up[l]], axis=-1).astype(BF16)
        ys = _moe(tile_elo, tile_ehi, n_used, xs, w_gu, w_down[l].astype(BF16))
        x2d = _final(dest, ys, x1, final_norm[None, :])
    return x2d.reshape(B, S, D)
```

```python
import functools

import jax
import jax.numpy as jnp
import numpy as np
from jax import lax
from jax.experimental import pallas as pl
from jax.experimental.pallas import tpu as pltpu

F32 = jnp.float32
BF16 = jnp.bfloat16

D_MODEL = 1024
SB_HEADS = 8
SB_HEAD_DIM = 64
SB_WIDTH = SB_HEADS * SB_HEAD_DIM
MLA_HEADS = 8
MLA_NOPE = 64
MLA_ROPE = 32
MLA_V = 64
MLA_WIDTH = MLA_HEADS * MLA_V
Q_LORA = 256
KV_LORA = 128
ROPE_BASE = 10000.0
N_GROUPS = 4
EXPERTS_PER_GROUP = 8
N_EXPERTS = N_GROUPS * EXPERTS_PER_GROUP
D_EXPERT = 256
EPS = 1e-6
LOG2_E = 1.4426950408889634

LANES = 128
SUBLANES = 8
N_PAIRS = EXPERTS_PER_GROUP * (EXPERTS_PER_GROUP - 1) // 2
N_BUCKETS = N_GROUPS * N_PAIRS
assert N_BUCKETS <= LANES
ROPE_HALF = MLA_ROPE // 2
HEAD_PAIRS = SB_HEADS // 2
assert SB_HEADS == MLA_HEADS and 2 * SB_HEAD_DIM == LANES and 2 * MLA_V == LANES

ROW_TILE = 512
ATT_TILE = 256
ATT_CHUNK = LANES
MXU_LEAD = 3
MOE_TILE = 256
EXT_WIDTH = D_MODEL + LANES
SB_UNDERFLOW = 151.0
VMEM_LIMIT = 56 * 1024 * 1024


def _rms(x):
    return x * lax.rsqrt(jnp.mean(x * x, axis=-1, keepdims=True) + EPS)


def _cparams(*sem):
    return pltpu.CompilerParams(dimension_semantics=sem, vmem_limit_bytes=VMEM_LIMIT)


def _proj_kernel(x_ref, pos_ref, invf_ref, an_ref, wsb_ref, wvsb_ref, wcq_ref, wckv_ref, wkr_ref, qn_ref, wuq_ref,
                 kvn_ref, wuk_ref, wuv_ref, qsb_ref, ksb_ref, vsb_ref, qm_ref, km_ref, vm_ref):
    tk = ATT_TILE
    nt_dims = (((1,), (1,)), ((), ()))
    hb = (_rms(x_ref[...]) * an_ref[...]).astype(BF16)
    sb = jnp.dot(hb, wsb_ref[...], preferred_element_type=F32)
    qsb_ref[...] = (sb[:, :SB_WIDTH] * (SB_HEAD_DIM ** -0.5 * LOG2_E)).astype(BF16)
    ksb_ref[...] = sb[:, SB_WIDTH:].astype(BF16)
    v_sb = lax.dot_general(wvsb_ref[...], hb, nt_dims, preferred_element_type=F32).astype(BF16)
    for kb in range(v_sb.shape[1] // tk):
        vsb_ref[kb] = v_sb[:, kb * tk:(kb + 1) * tk]

    cq = jnp.dot(hb, wcq_ref[...], preferred_element_type=F32)
    ckv = jnp.dot(hb, wckv_ref[...], preferred_element_type=F32)
    kr = jnp.dot(hb, wkr_ref[...], preferred_element_type=F32)
    q = jnp.dot((_rms(cq) * qn_ref[...]).astype(BF16), wuq_ref[...], preferred_element_type=F32)
    ckn = (_rms(ckv) * kvn_ref[...]).astype(BF16)
    kn = jnp.dot(ckn, wuk_ref[...], preferred_element_type=F32)
    v_m = lax.dot_general(wuv_ref[...], ckn, nt_dims, preferred_element_type=F32)
    ones_rows = (lax.broadcasted_iota(jnp.int32, (MLA_HEADS * LANES, 1), 0) % LANES) >= MLA_V
    v_m = jnp.where(ones_rows, 1.0, v_m).astype(BF16)
    for kb in range(v_m.shape[1] // tk):
        vm_ref[kb] = v_m[:, kb * tk:(kb + 1) * tk]

    ang = pos_ref[...].astype(F32) * invf_ref[...]
    cos, sin = jnp.cos(ang), jnp.sin(ang)
    lane = lax.broadcasted_iota(jnp.int32, (1, LANES), 1)
    x1_lanes = (lane >= MLA_NOPE) & (lane < MLA_NOPE + ROPE_HALF)
    x2_lanes = (lane >= MLA_NOPE + ROPE_HALF) & (lane < MLA_NOPE + MLA_ROPE)
    c_tab = jnp.where(lane < MLA_NOPE, 1.0, jnp.where(x1_lanes | x2_lanes, cos, 0.0))
    s_from_x2 = jnp.where(x1_lanes, -sin, 0.0)
    s_from_x1 = jnp.where(x2_lanes, sin, 0.0)

    def rope(t):
        return (t * c_tab + pltpu.roll(t, LANES - ROPE_HALF, 1) * s_from_x2
                + pltpu.roll(t, ROPE_HALF, 1) * s_from_x1)

    k_rope = rope(kr)
    q_scale = (MLA_NOPE + MLA_ROPE) ** -0.5 * LOG2_E
    for h in range(MLA_HEADS):
        blk = slice(h * LANES, (h + 1) * LANES)
        qm_ref[:, blk] = (rope(q[:, blk]) * q_scale).astype(BF16)
        km_ref[:, blk] = (kn[:, blk] + k_rope).astype(BF16)


def _projections(x2d, pos2d, invf, an, w):
    T = x2d.shape[0]
    tm = ROW_TILE
    tk = ATT_TILE
    row = lambda n: pl.BlockSpec((tm, n), lambda i: (i, 0))
    slab = lambda n: pl.BlockSpec((tm // tk, n, tk), lambda i: (i, 0, 0))
    full = lambda a: pl.BlockSpec(a.shape, lambda i: (0,) * a.ndim)
    ins = [x2d, pos2d, invf, an, w["sb"], w["vsb"], w["cq"], w["ckv"], w["kr"], w["qn"], w["uq"], w["kvn"],
           w["uk"], w["uv"]]
    in_specs = [row(D_MODEL), row(1)] + [full(a) for a in ins[2:]]
    wide = MLA_HEADS * LANES
    row_out = lambda n: jax.ShapeDtypeStruct((T, n), BF16)
    slab_out = lambda n: jax.ShapeDtypeStruct((T // tk, n, tk), BF16)
    return pl.pallas_call(
        _proj_kernel,
        grid=(T // tm,),
        in_specs=in_specs,
        out_specs=[row(SB_WIDTH), row(SB_WIDTH), slab(SB_WIDTH), row(wide), row(wide), slab(wide)],
        out_shape=[row_out(SB_WIDTH), row_out(SB_WIDTH), slab_out(SB_WIDTH), row_out(wide), row_out(wide),
                   slab_out(wide)],
        compiler_params=_cparams("parallel"),
        name="proj",
    )(*ins)


def _softplus2(z):
    sign_bit = jnp.uint32(0x80000000)
    neg_abs = lax.bitcast_convert_type(lax.bitcast_convert_type(z, jnp.uint32) | sign_bit, F32)
    return jnp.maximum(z, 0.0) + jnp.log2(1.0 + jnp.exp2(neg_abs))


def _sb_kernel(q_ref, k_ref, v_ref, o_ref, z_ref, w_ref, acc_ref, c_ref, scale_ref):
    tq = tk = ATT_TILE
    cw = ATT_CHUNK
    i = pl.program_id(1)
    lane = lax.broadcasted_iota(jnp.int32, (1, LANES), 1)
    in_head = (lane < SB_HEAD_DIM, lane >= SB_HEAD_DIM)
    later = (lax.broadcasted_iota(jnp.int32, (tk, tk), 1)
             > lax.broadcasted_iota(jnp.int32, (tk, tk), 0)).astype(BF16)
    key_idx = lax.broadcasted_iota(jnp.int32, (tk, cw), 0)
    qry_idx = lax.broadcasted_iota(jnp.int32, (tk, cw), 1)
    strict = [key_idx < qry_idx + c * cw for c in range(tq // cw)]
    nt_dims = (((1,), (1,)), ((), ()))
    pair_lanes = lambda h: slice((h // 2) * LANES, (h // 2 + 1) * LANES)

    def add_values(j_blk, h):
        v_head = v_ref[j_blk, h * SB_HEAD_DIM:(h + 1) * SB_HEAD_DIM, :]
        acc_ref[h] += jnp.dot(v_head, w_ref[h], preferred_element_type=F32) * scale_ref[h]

    def visit(j, j_prev, diagonal):
        keys = pl.ds(pl.multiple_of(j * tk, tk), tk)
        for h in range(SB_HEADS):
            q_pair = q_ref[:, pair_lanes(h)]
            qh = jnp.where(in_head[h % 2], q_pair, jnp.zeros_like(q_pair))
            z_ref[h] = lax.dot_general(k_ref[keys, pair_lanes(h)], qh, nt_dims, preferred_element_type=F32)
            if diagonal:
                acc_ref[h] = jnp.zeros((SB_HEAD_DIM, tq), F32)
            else:
                add_values(j_prev, h)
        for h in range(SB_HEADS):
            for c in range(tq // cw):
                cols = slice(c * cw, (c + 1) * cw)
                nk = min(tk, (c + 1) * cw) if diagonal else tk
                z = z_ref[h, :nk, cols]
                sp = _softplus2(z)
                if diagonal:
                    sp = jnp.where(strict[c][:nk], sp, 0.0)
                    if nk < tk:
                        w_ref[h, nk:, cols] = jnp.zeros((tk - nk, cw), BF16)
                z_ref[h, :nk, cols] = z - sp
                w_ref[h, :nk, cols] = sp.astype(BF16)
        c_low = None
        for h in range(SB_HEADS):
            suffix = jnp.dot(later, w_ref[h], preferred_element_type=F32)
            block_sum = suffix[0:1, :] + w_ref[h, 0:1, :].astype(F32)
            if diagonal:
                scale_ref[h] = jnp.ones((1, tq), F32)
                c_new = block_sum
            else:
                c_old = c_ref[h]
                scale_ref[h] = jnp.exp2(-c_old)
                c_new = c_old + block_sum
            c_ref[h] = c_new
            c_low = c_new if c_low is None else jnp.minimum(c_low, c_new)
            for c in range(tq // cw):
                cols = slice(c * cw, (c + 1) * cw)
                nk = min(tk, (c + 1) * cw) if diagonal else tk
                a = jnp.exp2(z_ref[h, :nk, cols] - suffix[:nk, cols])
                if diagonal:
                    a = jnp.where(strict[c][:nk], a, 0.0)
                w_ref[h, :nk, cols] = a.astype(BF16)
        return jnp.min(c_low)

    def cond(carry):
        j, _, c_min = carry
        return (j >= 0) & (c_min < SB_UNDERFLOW)

    def body(carry):
        j, j_prev, _ = carry
        return j - 1, j, visit(j, j_prev, diagonal=False)

    _, j_last, _ = lax.while_loop(cond, body, (i - 1, i, visit(i, i, diagonal=True)))
    for h in range(SB_HEADS):
        add_values(j_last, h)
    for p in range(HEAD_PAIRS):
        o_ref[:, p * LANES:(p + 1) * LANES] = jnp.concatenate([acc_ref[2 * p], acc_ref[2 * p + 1]], axis=0).T


def _sb_attention(q, k, v, B, S):
    tq = tk = ATT_TILE
    nq = S // tq
    return pl.pallas_call(
        _sb_kernel,
        grid=(B, nq),
        in_specs=[
            pl.BlockSpec((tq, SB_WIDTH), lambda b, i: (b * nq + i, 0)),
            pl.BlockSpec((S, SB_WIDTH), lambda b, i: (b, 0)),
            pl.BlockSpec((S // tk, SB_WIDTH, tk), lambda b, i: (b, 0, 0)),
        ],
        out_specs=pl.BlockSpec((tq, SB_WIDTH), lambda b, i: (b * nq + i, 0)),
        out_shape=jax.ShapeDtypeStruct((B * S, SB_WIDTH), F32),
        scratch_shapes=[
            pltpu.VMEM((SB_HEADS, tk, tq), F32),
            pltpu.VMEM((SB_HEADS, tk, tq), BF16),
            pltpu.VMEM((SB_HEADS, SB_HEAD_DIM, tq), F32),
            pltpu.VMEM((SB_HEADS, 1, tq), F32),
            pltpu.VMEM((SB_HEADS, 1, tq), F32),
        ],
        compiler_params=_cparams("parallel", "parallel"),
        name="sb_attn",
    )(q, k, v)


def _mla_kernel(q_ref, k_ref, v_ref, o_ref, s_ref, p_ref, acc_ref, m_ref):
    tq = tk = ATT_TILE
    cw = ATT_CHUNK
    i = pl.program_id(1)
    key_idx = lax.broadcasted_iota(jnp.int32, (tk, cw), 0)
    qry_idx = lax.broadcasted_iota(jnp.int32, (tk, cw), 1)
    causal = [key_idx <= qry_idx + c * cw for c in range(tq // cw)]
    nt_dims = (((1,), (1,)), ((), ()))
    head_lanes = lambda h: slice(h * LANES, (h + 1) * LANES)

    def visit(j, j_prev, diagonal):
        keys = pl.ds(pl.multiple_of(j * tk, tk), tk)

        def scores(h):
            s_ref[h] = lax.dot_general(k_ref[keys, head_lanes(h)], q_ref[:, head_lanes(h)], nt_dims,
                                       preferred_element_type=F32)

        for h in range(MXU_LEAD):
            scores(h)
        for h in range(MLA_HEADS):
            if diagonal:
                acc_ref[h] = jnp.zeros((LANES, tq), F32)
            else:
                acc_ref[h] += jnp.dot(v_ref[j_prev, head_lanes(h), :], p_ref[h], preferred_element_type=F32)
            if h + MXU_LEAD < MLA_HEADS:
                scores(h + MXU_LEAD)
        for h in range(MLA_HEADS):
            for c in range(tq // cw):
                cols = slice(c * cw, (c + 1) * cw)
                nk = min(tk, (c + 1) * cw) if diagonal else tk
                s = s_ref[h, :nk, cols]
                if diagonal:
                    s = jnp.where(causal[c][:nk], s, -jnp.inf)
                    m_new = jnp.max(s, axis=0, keepdims=True)
                    if nk < tk:
                        p_ref[h, nk:, cols] = jnp.zeros((tk - nk, cw), BF16)
                else:
                    m_old = m_ref[h, :, cols]
                    m_new = jnp.maximum(m_old, jnp.max(s, axis=0, keepdims=True))
                    acc_ref[h, :, cols] *= jnp.exp2(m_old - m_new)
                p_ref[h, :nk, cols] = jnp.exp2(s - m_new).astype(BF16)
                m_ref[h, :, cols] = m_new

    visit(i, i, diagonal=True)

    def body(j, carry):
        visit(j, jnp.where(j == 0, i, j - 1), diagonal=False)
        return carry

    lax.fori_loop(0, i, body, 0)
    j_last = jnp.where(i == 0, i, i - 1)
    for h in range(MLA_HEADS):
        acc_ref[h] += jnp.dot(v_ref[j_last, head_lanes(h), :], p_ref[h], preferred_element_type=F32)
    for p in range(HEAD_PAIRS):
        outs = [acc_ref[2 * p + e, :MLA_V] / acc_ref[2 * p + e, MLA_V:] for e in range(2)]
        o_ref[:, p * LANES:(p + 1) * LANES] = jnp.concatenate(outs, axis=0).T


def _mla_attention(q, k, v, B, S):
    tq = tk = ATT_TILE
    nq = S // tq
    width = MLA_HEADS * LANES
    return pl.pallas_call(
        _mla_kernel,
        grid=(B, nq),
        in_specs=[
            pl.BlockSpec((tq, width), lambda b, i: (b * nq + i, 0)),
            pl.BlockSpec((S, width), lambda b, i: (b, 0)),
            pl.BlockSpec((S // tk, width, tk), lambda b, i: (b, 0, 0)),
        ],
        out_specs=pl.BlockSpec((tq, MLA_WIDTH), lambda b, i: (b * nq + i, 0)),
        out_shape=jax.ShapeDtypeStruct((B * S, MLA_WIDTH), F32),
        scratch_shapes=[
            pltpu.VMEM((MLA_HEADS, tk, tq), F32),
            pltpu.VMEM((MLA_HEADS, tk, tq), BF16),
            pltpu.VMEM((MLA_HEADS, LANES, tq), F32),
            pltpu.VMEM((MLA_HEADS, 1, tq), F32),
        ],
        compiler_params=_cparams("parallel", "parallel"),
        name="mla_attn",
    )(q, k, v)


ROUTER_ROWS = SUBLANES * (1 + N_GROUPS)


def _split_bf16(a):
    hi = a.astype(BF16)
    return hi, (a - hi.astype(F32)).astype(BF16)


def _route_kernel(osb_ref, omla_ref, x_ref, g_sb_ref, g_mla_ref, wout_ref, fn_ref, wr_hi_ref, wr_lo_ref, br_ref,
                  x1_ref, hext_ref, bucket_ref, rank_ref, counts_ref, carry_ref):
    tm = ROW_TILE

    @pl.when(pl.program_id(0) == 0)
    def _():
        carry_ref[...] = jnp.zeros_like(carry_ref)

    o = jnp.concatenate([_rms(osb_ref[...]) * g_sb_ref[...], _rms(omla_ref[...]) * g_mla_ref[...]], axis=-1)
    x1 = x_ref[...] + jnp.dot(o.astype(BF16), wout_ref[...], preferred_element_type=F32)
    x1_ref[...] = x1
    h = _rms(x1) * fn_ref[...]
    hext_ref[:, :D_MODEL] = h

    h_hi, h_lo = _split_bf16(h)
    logits = (jnp.dot(h_hi, wr_hi_ref[...], preferred_element_type=F32)
              + jnp.dot(h_hi, wr_lo_ref[...], preferred_element_type=F32)
              + jnp.dot(h_lo, wr_hi_ref[...], preferred_element_type=F32)) + br_ref[...]
    lt = logits.T

    rid = lax.broadcasted_iota(jnp.int32, (SUBLANES, tm), 0)
    g_logit = jnp.where(rid < N_GROUPS, lt[:SUBLANES], -jnp.inf)
    g_exp = jnp.exp(g_logit - jnp.max(g_logit, axis=0, keepdims=True))
    p_group = g_exp / jnp.sum(g_exp, axis=0, keepdims=True)
    g_val = jnp.max(p_group, axis=0, keepdims=True)
    g_idx = jnp.min(jnp.where(p_group == g_val, rid, SUBLANES), axis=0, keepdims=True)
    local = lt[SUBLANES * N_GROUPS:SUBLANES * (N_GROUPS + 1)]
    for g in range(N_GROUPS - 2, -1, -1):
        local = jnp.where(g_idx == g, lt[SUBLANES * (g + 1):SUBLANES * (g + 2)], local)
    e_exp = jnp.exp(local - jnp.max(local, axis=0, keepdims=True))
    p_exp = e_exp / jnp.sum(e_exp, axis=0, keepdims=True)
    v1 = jnp.max(p_exp, axis=0, keepdims=True)
    i1 = jnp.min(jnp.where(p_exp == v1, rid, SUBLANES), axis=0, keepdims=True)
    rest = jnp.where(rid == i1, -1.0, p_exp)
    v2 = jnp.max(rest, axis=0, keepdims=True)
    i2 = jnp.min(jnp.where(rest == v2, rid, SUBLANES), axis=0, keepdims=True)
    den = v1 + v2
    w1 = g_val * v1 / den
    w2 = g_val * v2 / den
    first_lower = i1 < i2
    e_lo = jnp.where(first_lower, i1, i2)
    e_hi = jnp.where(first_lower, i2, i1)
    w_lo = jnp.where(first_lower, w1, w2)
    w_hi = jnp.where(first_lower, w2, w1)
    pair = ((e_lo * (2 * EXPERTS_PER_GROUP - 1 - e_lo)) >> 1) + (e_hi - e_lo - 1)
    bucket = g_idx * N_PAIRS + pair
    bucket_ref[0] = bucket

    rid_full = lax.broadcasted_iota(jnp.int32, (LANES, tm), 0)
    w_rows = jnp.where(rid_full == 0, w_lo, jnp.where(rid_full == 1, w_hi, 0.0))
    hext_ref[:, D_MODEL:] = w_rows.T

    onehot = (rid_full == bucket).astype(F32)
    trow = lax.broadcasted_iota(jnp.int32, (tm, tm), 0)
    tcol = lax.broadcasted_iota(jnp.int32, (tm, tm), 1)
    earlier = (trow < tcol).astype(BF16)
    before = jnp.dot(onehot.astype(BF16), earlier, preferred_element_type=F32) + carry_ref[...]
    rank_ref[0] = jnp.sum(onehot * before, axis=0, keepdims=True).astype(jnp.int32)
    carry_ref[...] += jnp.sum(onehot, axis=1, keepdims=True)
    counts_ref[...] = carry_ref[...]


def _route(o_sb, o_mla, x2d, g_sb, g_mla, w_out, fn, wr_hi, wr_lo, br):
    T = x2d.shape[0]
    tm = ROW_TILE
    nt = T // tm
    row = lambda n: pl.BlockSpec((tm, n), lambda i: (i, 0))
    full = lambda a: pl.BlockSpec(a.shape, lambda i: (0,) * a.ndim)
    tok = pl.BlockSpec((1, 1, tm), lambda i: (i, 0, 0))
    ins = [o_sb, o_mla, x2d, g_sb, g_mla, w_out, fn, wr_hi, wr_lo, br]
    return pl.pallas_call(
        _route_kernel,
        grid=(nt,),
        in_specs=[row(SB_WIDTH), row(MLA_WIDTH), row(D_MODEL)] + [full(a) for a in ins[3:]],
        out_specs=[row(D_MODEL), row(EXT_WIDTH), tok, tok, pl.BlockSpec((LANES, 1), lambda i: (0, 0))],
        out_shape=[
            jax.ShapeDtypeStruct((T, D_MODEL), F32),
            jax.ShapeDtypeStruct((T, EXT_WIDTH), F32),
            jax.ShapeDtypeStruct((nt, 1, tm), jnp.int32),
            jax.ShapeDtypeStruct((nt, 1, tm), jnp.int32),
            jax.ShapeDtypeStruct((LANES, 1), F32),
        ],
        scratch_shapes=[pltpu.VMEM((LANES, 1), F32)],
        compiler_params=_cparams("arbitrary"),
        name="route",
    )(*ins)


def _dest_kernel(bucket_ref, rank_ref, offs_ref, dest_ref):
    tm = bucket_ref.shape[-1]
    rid = lax.broadcasted_iota(jnp.int32, (LANES, tm), 0)
    start = jnp.sum(jnp.where(rid == bucket_ref[0], offs_ref[...], 0), axis=0, keepdims=True)
    dest_ref[0] = start + rank_ref[0]


def _dest_rows(bucket, rank, offsets):
    nt, _, tm = bucket.shape
    tok = pl.BlockSpec((1, 1, tm), lambda i: (i, 0, 0))
    return pl.pallas_call(
        _dest_kernel,
        grid=(nt,),
        in_specs=[tok, tok, pl.BlockSpec((LANES, 1), lambda i: (0, 0))],
        out_specs=tok,
        out_shape=jax.ShapeDtypeStruct((nt, 1, tm), jnp.int32),
        compiler_params=_cparams("parallel"),
        name="dest",
    )(bucket, rank, offsets)


def _scatter_kernel(dest_ref, h_ref, init_ref, xs_ref, sem):
    del init_ref
    tm = h_ref.shape[0]
    base = pl.program_id(0) * tm

    def issue(r, _):
        d = dest_ref[base + r]
        pltpu.make_async_copy(h_ref.at[pl.ds(r, 1)], xs_ref.at[pl.ds(d, 1)], sem).start()
        return _

    lax.fori_loop(0, tm, issue, 0, unroll=8)
    pltpu.make_async_copy(h_ref, xs_ref.at[pl.ds(0, tm)], sem).wait()


def _scatter_rows(dest, hext, n_rows):
    T, W = hext.shape
    tm = ROW_TILE
    init = jnp.zeros((n_rows, W), hext.dtype)
    return pl.pallas_call(
        _scatter_kernel,
        grid_spec=pltpu.PrefetchScalarGridSpec(
            num_scalar_prefetch=1,
            grid=(T // tm,),
            in_specs=[pl.BlockSpec((tm, W), lambda i, d: (i, 0)), pl.BlockSpec(memory_space=pl.ANY)],
            out_specs=pl.BlockSpec(memory_space=pl.ANY),
            scratch_shapes=[pltpu.SemaphoreType.DMA],
        ),
        out_shape=jax.ShapeDtypeStruct((n_rows, W), hext.dtype),
        input_output_aliases={2: 0},
        compiler_params=_cparams("arbitrary"),
        name="scatter",
    )(dest, hext, init)


def _moe_kernel(elo_ref, ehi_ref, nt_ref, xs_ref, wgu_lo_ref, wd_lo_ref, wgu_hi_ref, wd_hi_ref, ys_ref):
    del elo_ref, ehi_ref
    used = pl.program_id(0) < nt_ref[0]

    @pl.when(jnp.logical_not(used))
    def _():
        ys_ref[...] = jnp.zeros_like(ys_ref)

    @pl.when(used)
    def _():
        h = xs_ref[:, :D_MODEL].astype(BF16)
        gates = xs_ref[:, D_MODEL:]

        def expert(wgu_ref, wd_ref):
            gu = jnp.dot(h, wgu_ref[0], preferred_element_type=F32)
            g, u = gu[:, :D_EXPERT], gu[:, D_EXPERT:]
            hid = (g * jax.nn.sigmoid(g)) * u
            return jnp.dot(hid.astype(BF16), wd_ref[0], preferred_element_type=F32)

        ys_ref[...] = (expert(wgu_lo_ref, wd_lo_ref) * gates[:, 0:1]
                       + expert(wgu_hi_ref, wd_hi_ref) * gates[:, 1:2])


def _moe(tile_elo, tile_ehi, n_tiles_used, xs, w_gu, w_d):
    n_rows = xs.shape[0]
    tile = MOE_TILE
    rows = lambda w: pl.BlockSpec((tile, w), lambda i, elo, ehi, nt: (jnp.minimum(i, nt[0] - 1), 0))
    wspec = lambda shape, which: pl.BlockSpec(
        (1,) + shape, lambda i, elo, ehi, nt: ((elo, ehi)[which][i], 0, 0))
    return pl.pallas_call(
        _moe_kernel,
        grid_spec=pltpu.PrefetchScalarGridSpec(
            num_scalar_prefetch=3,
            grid=(n_rows // tile,),
            in_specs=[
                rows(EXT_WIDTH),
                wspec((D_MODEL, 2 * D_EXPERT), 0), wspec((D_EXPERT, D_MODEL), 0),
                wspec((D_MODEL, 2 * D_EXPERT), 1), wspec((D_EXPERT, D_MODEL), 1),
            ],
            out_specs=pl.BlockSpec((tile, D_MODEL), lambda i, elo, ehi, nt: (i, 0)),
        ),
        out_shape=jax.ShapeDtypeStruct((n_rows, D_MODEL), F32),
        compiler_params=_cparams("arbitrary"),
        name="moe",
    )(tile_elo, tile_ehi, n_tiles_used, xs, w_gu, w_d, w_gu, w_d)


def _final_kernel(dest_ref, ys_ref, x1_ref, fn_ref, o_ref, buf_ref, sem):
    tm = x1_ref.shape[0]
    base = pl.program_id(0) * tm

    def issue(r, _):
        d = dest_ref[base + r]
        pltpu.make_async_copy(ys_ref.at[pl.ds(d, 1)], buf_ref.at[pl.ds(r, 1)], sem).start()
        return _

    lax.fori_loop(0, tm, issue, 0, unroll=8)
    pltpu.make_async_copy(ys_ref.at[pl.ds(0, tm)], buf_ref, sem).wait()
    o_ref[...] = _rms(x1_ref[...] + buf_ref[...]) * fn_ref[...]


def _final(dest, ys, x1, fn):
    T, D = x1.shape
    tm = ROW_TILE
    return pl.pallas_call(
        _final_kernel,
        grid_spec=pltpu.PrefetchScalarGridSpec(
            num_scalar_prefetch=1,
            grid=(T // tm,),
            in_specs=[
                pl.BlockSpec(memory_space=pl.ANY),
                pl.BlockSpec((tm, D), lambda i, d: (i, 0)),
                pl.BlockSpec((1, D), lambda i, d: (0, 0)),
            ],
            out_specs=pl.BlockSpec((tm, D), lambda i, d: (i, 0)),
            scratch_shapes=[pltpu.VMEM((tm, D), F32), pltpu.SemaphoreType.DMA],
        ),
        out_shape=jax.ShapeDtypeStruct((T, D), F32),
        compiler_params=_cparams("arbitrary"),
        name="final",
    )(dest, ys, x1, fn)


def _pair_tables():
    lo, hi = [], []
    for g in range(N_GROUPS):
        for a in range(EXPERTS_PER_GROUP):
            for b in range(a + 1, EXPERTS_PER_GROUP):
                lo.append(g * EXPERTS_PER_GROUP + a)
                hi.append(g * EXPERTS_PER_GROUP + b)
    return np.asarray(lo, np.int32), np.asarray(hi, np.int32)


def _attention_weights(w_in, q_norm, w_uq, kv_norm, w_ukv):
    D = w_in.shape[0]
    c0 = 3 * SB_WIDTH
    zeros = lambda r, c: jnp.zeros((r, c), F32)
    w_kr = jnp.concatenate(
        [zeros(D, MLA_NOPE), w_in[:, c0 + Q_LORA + KV_LORA:], zeros(D, LANES - MLA_NOPE - MLA_ROPE)], axis=1)
    dq = MLA_NOPE + MLA_ROPE
    uq = jnp.concatenate(
        [jnp.concatenate([w_uq[:, h * dq:(h + 1) * dq], zeros(Q_LORA, LANES - dq)], axis=1)
         for h in range(MLA_HEADS)], axis=1)
    dkv = MLA_NOPE + MLA_V
    uk = jnp.concatenate(
        [jnp.concatenate([w_ukv[:, h * dkv:h * dkv + MLA_NOPE], zeros(KV_LORA, LANES - MLA_NOPE)], axis=1)
         for h in range(MLA_HEADS)], axis=1)
    uv = jnp.concatenate(
        [jnp.concatenate([w_ukv[:, h * dkv + MLA_NOPE:(h + 1) * dkv].T, zeros(LANES - MLA_V, KV_LORA)], axis=0)
         for h in range(MLA_HEADS)], axis=0)
    return {
        "sb": w_in[:, :2 * SB_WIDTH].astype(BF16),
        "vsb": w_in[:, 2 * SB_WIDTH:c0].T.astype(BF16),
        "cq": w_in[:, c0:c0 + Q_LORA].astype(BF16),
        "ckv": w_in[:, c0 + Q_LORA:c0 + Q_LORA + KV_LORA].astype(BF16),
        "kr": w_kr.astype(BF16),
        "qn": q_norm[None, :],
        "uq": uq.astype(BF16),
        "kvn": kv_norm[None, :],
        "uk": uk.astype(BF16),
        "uv": uv.astype(BF16),
    }


def _router_weights(w_group, b_group, w_expert, b_expert):
    D = w_group.shape[0]
    pad_g = SUBLANES - N_GROUPS
    pad_e = LANES - SUBLANES - N_EXPERTS
    w = jnp.concatenate([w_group, jnp.zeros((D, pad_g), F32), w_expert, jnp.zeros((D, pad_e), F32)], axis=1)
    b = jnp.concatenate([b_group, jnp.zeros((pad_g,), F32), b_expert, jnp.zeros((pad_e,), F32)])[None, :]
    w_hi, w_lo = _split_bf16(w)
    return w_hi, w_lo, b


def _bucket_layout(counts, n_tiles):
    c = counts[:N_BUCKETS, 0].astype(jnp.int32)
    tiles = (c + MOE_TILE - 1) // MOE_TILE
    tile_end = jnp.cumsum(tiles)
    offsets = (tile_end - tiles) * MOE_TILE
    offsets = jnp.concatenate([offsets, jnp.zeros((LANES - N_BUCKETS,), jnp.int32)])[:, None]
    n_used = tile_end[-1]
    tile_id = jnp.minimum(jnp.arange(n_tiles, dtype=jnp.int32), n_used - 1)
    tile_bucket = jnp.searchsorted(tile_end, tile_id, side="right").astype(jnp.int32)
    pair_lo, pair_hi = _pair_tables()
    return offsets, jnp.asarray(pair_lo)[tile_bucket], jnp.asarray(pair_hi)[tile_bucket], n_used[None]


def kernel(x, positions, attn_norm, w_in, q_norm, w_uq, kv_norm, w_ukv, sb_out_norm, mla_out_norm, w_out,
           ffn_norm, w_group_router, b_group_router, w_expert_router, b_expert_router, w_gate, w_up, w_down,
           final_norm):
    B, S, D = x.shape
    T = B * S
    depth = w_in.shape[0]
    assert D == D_MODEL and T % ROW_TILE == 0 and S % ATT_TILE == 0
    assert depth == 1, "the final norm is fused into the last layer's gather kernel"
    n_sorted_tiles = T // MOE_TILE + N_BUCKETS
    n_sorted_rows = n_sorted_tiles * MOE_TILE

    lane = jnp.arange(LANES)
    invf = (ROPE_BASE ** (-(lane % ROPE_HALF).astype(F32) / ROPE_HALF))[None, :]
    pos2d = positions.reshape(T, 1)
    x2d = x.reshape(T, D)
    for l in range(depth):
        aw = _attention_weights(w_in[l], q_norm[l], w_uq[l], kv_norm[l], w_ukv[l])
        q_sb, k_sb, v_sb, q_m, k_m, v_m = _projections(x2d, pos2d, invf, attn_norm[l][None, :], aw)
        o_sb = _sb_attention(q_sb, k_sb, v_sb, B, S)
        o_mla = _mla_attention(q_m, k_m, v_m, B, S)

        wr_hi, wr_lo, br = _router_weights(w_group_router[l], b_group_router[l], w_expert_router[l], b_expert_router[l])
        x1, hext, bucket, rank, counts = _route(
            o_sb, o_mla, x2d, sb_out_norm[l][None, :], mla_out_norm[l][None, :], w_out[l].astype(BF16),
            ffn_norm[l][None, :], wr_hi, wr_lo, br)
        offsets, tile_elo, tile_ehi, n_used = _bucket_layout(counts, n_sorted_tiles)
        dest = _dest_rows(bucket, rank, offsets).reshape(T)

        xs = _scatter_rows(dest, hext, n_sorted_rows)
        w_gu = jnp.concatenate([w_gate[l], w_up[l]], axis=-1).astype(BF16)
        ys = _moe(tile_elo, tile_ehi, n_used, xs, w_gu, w_down[l].astype(BF16))
        x2d = _final(dest, ys, x1, final_norm[None, :])
    return x2d.reshape(B, S, D)
```

```python
import functools

import jax
import jax.numpy as jnp
import numpy as np
from jax import lax
from jax.experimental import pallas as pl
from jax.experimental.pallas import tpu as pltpu

F32 = jnp.float32
BF16 = jnp.bfloat16

D_MODEL = 1024
SB_HEADS = 8
SB_HEAD_DIM = 64
SB_WIDTH = SB_HEADS * SB_HEAD_DIM
MLA_HEADS = 8
MLA_NOPE = 64
MLA_ROPE = 32
MLA_V = 64
MLA_WIDTH = MLA_HEADS * MLA_V
Q_LORA = 256
KV_LORA = 128
ROPE_BASE = 10000.0
N_GROUPS = 4
EXPERTS_PER_GROUP = 8
N_EXPERTS = N_GROUPS * EXPERTS_PER_GROUP
D_EXPERT = 256
EPS = 1e-6
LOG2_E = 1.4426950408889634

LANES = 128
SUBLANES = 8
N_PAIRS = EXPERTS_PER_GROUP * (EXPERTS_PER_GROUP - 1) // 2
N_BUCKETS = N_GROUPS * N_PAIRS
assert N_BUCKETS <= LANES
ROPE_HALF = MLA_ROPE // 2
HEAD_PAIRS = SB_HEADS // 2
assert SB_HEADS == MLA_HEADS and 2 * SB_HEAD_DIM == LANES and 2 * MLA_V == LANES

ROW_TILE = 512
ATT_TILE = 256
ATT_CHUNK = LANES
MXU_LEAD = 3
MOE_TILE = 256
EXT_WIDTH = D_MODEL + LANES
SB_UNDERFLOW = 151.0
VMEM_LIMIT = 56 * 1024 * 1024


def _rms(x):
    return x * lax.rsqrt(jnp.mean(x * x, axis=-1, keepdims=True) + EPS)


def _cparams(*sem):
    return pltpu.CompilerParams(dimension_semantics=sem, vmem_limit_bytes=VMEM_LIMIT)


def _proj_kernel(x_ref, pos_ref, invf_ref, an_ref, wsb_ref, wvsb_ref, wcq_ref, wckv_ref, wkr_ref, qn_ref, wuq_ref,
                 kvn_ref, wuk_ref, wuv_ref, qsb_ref, ksb_ref, vsb_ref, qm_ref, km_ref, vm_ref):
    tk = ATT_TILE
    nt_dims = (((1,), (1,)), ((), ()))
    hb = (_rms(x_ref[...]) * an_ref[...]).astype(BF16)
    sb = jnp.dot(hb, wsb_ref[...], preferred_element_type=F32)
    qsb_ref[...] = (sb[:, :SB_WIDTH] * (SB_HEAD_DIM ** -0.5 * LOG2_E)).astype(BF16)
    ksb_ref[...] = sb[:, SB_WIDTH:].astype(BF16)
    v_sb = lax.dot_general(wvsb_ref[...], hb, nt_dims, preferred_element_type=F32).astype(BF16)
    for kb in range(v_sb.shape[1] // tk):
        vsb_ref[kb] = v_sb[:, kb * tk:(kb + 1) * tk]

    cq = jnp.dot(hb, wcq_ref[...], preferred_element_type=F32)
    ckv = jnp.dot(hb, wckv_ref[...], preferred_element_type=F32)
    kr = jnp.dot(hb, wkr_ref[...], preferred_element_type=F32)
    q = jnp.dot((_rms(cq) * qn_ref[...]).astype(BF16), wuq_ref[...], preferred_element_type=F32)
    ckn = (_rms(ckv) * kvn_ref[...]).astype(BF16)
    kn = jnp.dot(ckn, wuk_ref[...], preferred_element_type=F32)
    v_m = lax.dot_general(wuv_ref[...], ckn, nt_dims, preferred_element_type=F32)
    ones_rows = (lax.broadcasted_iota(jnp.int32, (MLA_HEADS * LANES, 1), 0) % LANES) >= MLA_V
    v_m = jnp.where(ones_rows, 1.0, v_m).astype(BF16)
    for kb in range(v_m.shape[1] // tk):
        vm_ref[kb] = v_m[:, kb * tk:(kb + 1) * tk]

    ang = pos_ref[...].astype(F32) * invf_ref[...]
    cos, sin = jnp.cos(ang), jnp.sin(ang)
    lane = lax.broadcasted_iota(jnp.int32, (1, LANES), 1)
    x1_lanes = (lane >= MLA_NOPE) & (lane < MLA_NOPE + ROPE_HALF)
    x2_lanes = (lane >= MLA_NOPE + ROPE_HALF) & (lane < MLA_NOPE + MLA_ROPE)
    c_tab = jnp.where(lane < MLA_NOPE, 1.0, jnp.where(x1_lanes | x2_lanes, cos, 0.0))
    s_from_x2 = jnp.where(x1_lanes, -sin, 0.0)
    s_from_x1 = jnp.where(x2_lanes, sin, 0.0)

    def rope(t):
        return (t * c_tab + pltpu.roll(t, LANES - ROPE_HALF, 1) * s_from_x2
                + pltpu.roll(t, ROPE_HALF, 1) * s_from_x1)

    k_rope = rope(kr)
    q_scale = (MLA_NOPE + MLA_ROPE) ** -0.5 * LOG2_E
    for h in range(MLA_HEADS):
        blk = slice(h * LANES, (h + 1) * LANES)
        qm_ref[:, blk] = (rope(q[:, blk]) * q_scale).astype(BF16)
        km_ref[:, blk] = (kn[:, blk] + k_rope).astype(BF16)


def _projections(x2d, pos2d, invf, an, w):
    T = x2d.shape[0]
    tm = ROW_TILE
    tk = ATT_TILE
    row = lambda n: pl.BlockSpec((tm, n), lambda i: (i, 0))
    slab = lambda n: pl.BlockSpec((tm // tk, n, tk), lambda i: (i, 0, 0))
    full = lambda a: pl.BlockSpec(a.shape, lambda i: (0,) * a.ndim)
    ins = [x2d, pos2d, invf, an, w["sb"], w["vsb"], w["cq"], w["ckv"], w["kr"], w["qn"], w["uq"], w["kvn"],
           w["uk"], w["uv"]]
    in_specs = [row(D_MODEL), row(1)] + [full(a) for a in ins[2:]]
    wide = MLA_HEADS * LANES
    row_out = lambda n: jax.ShapeDtypeStruct((T, n), BF16)
    slab_out = lambda n: jax.ShapeDtypeStruct((T // tk, n, tk), BF16)
    return pl.pallas_call(
        _proj_kernel,
        grid=(T // tm,),
        in_specs=in_specs,
        out_specs=[row(SB_WIDTH), row(SB_WIDTH), slab(SB_WIDTH), row(wide), row(wide), slab(wide)],
        out_shape=[row_out(SB_WIDTH), row_out(SB_WIDTH), slab_out(SB_WIDTH), row_out(wide), row_out(wide),
                   slab_out(wide)],
        compiler_params=_cparams("parallel"),
        name="proj",
    )(*ins)


def _softplus2(z):
    sign_bit = jnp.uint32(0x80000000)
    neg_abs = lax.bitcast_convert_type(lax.bitcast_convert_type(z, jnp.uint32) | sign_bit, F32)
    return jnp.maximum(z, 0.0) + jnp.log2(1.0 + jnp.exp2(neg_abs))


def _sb_kernel(q_ref, k_ref, v_ref, o_ref, z_ref, w_ref, acc_ref, c_ref, scale_ref):
    tq = tk = ATT_TILE
    cw = ATT_CHUNK
    i = pl.program_id(1)
    lane = lax.broadcasted_iota(jnp.int32, (1, LANES), 1)
    in_head = (lane < SB_HEAD_DIM, lane >= SB_HEAD_DIM)
    later = (lax.broadcasted_iota(jnp.int32, (tk, tk), 1)
             > lax.broadcasted_iota(jnp.int32, (tk, tk), 0)).astype(BF16)
    key_idx = lax.broadcasted_iota(jnp.int32, (tk, cw), 0)
    qry_idx = lax.broadcasted_iota(jnp.int32, (tk, cw), 1)
    strict = [key_idx < qry_idx + c * cw for c in range(tq // cw)]
    nt_dims = (((1,), (1,)), ((), ()))
    pair_lanes = lambda h: slice((h // 2) * LANES, (h // 2 + 1) * LANES)

    def add_values(j_blk, h):
        v_head = v_ref[j_blk, h * SB_HEAD_DIM:(h + 1) * SB_HEAD_DIM, :]
        acc_ref[h] += jnp.dot(v_head, w_ref[h], preferred_element_type=F32) * scale_ref[h]

    def visit(j, j_prev, diagonal):
        keys = pl.ds(pl.multiple_of(j * tk, tk), tk)
        for h in range(SB_HEADS):
            q_pair = q_ref[:, pair_lanes(h)]
            qh = jnp.where(in_head[h % 2], q_pair, jnp.zeros_like(q_pair))
            z_ref[h] = lax.dot_general(k_ref[keys, pair_lanes(h)], qh, nt_dims, preferred_element_type=F32)
            if diagonal:
                acc_ref[h] = jnp.zeros((SB_HEAD_DIM, tq), F32)
            else:
                add_values(j_prev, h)
        for h in range(SB_HEADS):
            for c in range(tq // cw):
                cols = slice(c * cw, (c + 1) * cw)
                nk = min(tk, (c + 1) * cw) if diagonal else tk
                z = z_ref[h, :nk, cols]
                sp = _softplus2(z)
                if diagonal:
                    sp = jnp.where(strict[c][:nk], sp, 0.0)
                    if nk < tk:
                        w_ref[h, nk:, cols] = jnp.zeros((tk - nk, cw), BF16)
                z_ref[h, :nk, cols] = z - sp
                w_ref[h, :nk, cols] = sp.astype(BF16)
        c_low = None
        for h in range(SB_HEADS):
            suffix = jnp.dot(later, w_ref[h], preferred_element_type=F32)
            block_sum = suffix[0:1, :] + w_ref[h, 0:1, :].astype(F32)
            if diagonal:
                scale_ref[h] = jnp.ones((1, tq), F32)
                c_new = block_sum
            else:
                c_old = c_ref[h]
                scale_ref[h] = jnp.exp2(-c_old)
                c_new = c_old + block_sum
            c_ref[h] = c_new
            c_low = c_new if c_low is None else jnp.minimum(c_low, c_new)
            for c in range(tq // cw):
                cols = slice(c * cw, (c + 1) * cw)
                nk = min(tk, (c + 1) * cw) if diagonal else tk
                a = jnp.exp2(z_ref[h, :nk, cols] - suffix[:nk, cols])
                if diagonal:
                    a = jnp.where(strict[c][:nk], a, 0.0)
                w_ref[h, :nk, cols] = a.astype(BF16)
        return jnp.min(c_low)

    def cond(carry):
        j, _, c_min = carry
        return (j >= 0) & (c_min < SB_UNDERFLOW)

    def body(carry):
        j, j_prev, _ = carry
        return j - 1, j, visit(j, j_prev, diagonal=False)

    _, j_last, _ = lax.while_loop(cond, body, (i - 1, i, visit(i, i, diagonal=True)))
    for h in range(SB_HEADS):
        add_values(j_last, h)
    for p in range(HEAD_PAIRS):
        o_ref[:, p * LANES:(p + 1) * LANES] = jnp.concatenate([acc_ref[2 * p], acc_ref[2 * p + 1]], axis=0).T


def _sb_attention(q, k, v, B, S):
    tq = tk = ATT_TILE
    nq = S // tq
    return pl.pallas_call(
        _sb_kernel,
        grid=(B, nq),
        in_specs=[
            pl.BlockSpec((tq, SB_WIDTH), lambda b, i: (b * nq + i, 0)),
            pl.BlockSpec((S, SB_WIDTH), lambda b, i: (b, 0)),
            pl.BlockSpec((S // tk, SB_WIDTH, tk), lambda b, i: (b, 0, 0)),
        ],
        out_specs=pl.BlockSpec((tq, SB_WIDTH), lambda b, i: (b * nq + i, 0)),
        out_shape=jax.ShapeDtypeStruct((B * S, SB_WIDTH), F32),
        scratch_shapes=[
            pltpu.VMEM((SB_HEADS, tk, tq), F32),
            pltpu.VMEM((SB_HEADS, tk, tq), BF16),
            pltpu.VMEM((SB_HEADS, SB_HEAD_DIM, tq), F32),
            pltpu.VMEM((SB_HEADS, 1, tq), F32),
            pltpu.VMEM((SB_HEADS, 1, tq), F32),
        ],
        compiler_params=_cparams("parallel", "parallel"),
        name="sb_attn",
    )(q, k, v)


def _mla_kernel(q_ref, k_ref, v_ref, o_ref, s_ref, p_ref, acc_ref, m_ref):
    tq = tk = ATT_TILE
    cw = ATT_CHUNK
    i = pl.program_id(1)
    key_idx = lax.broadcasted_iota(jnp.int32, (tk, cw), 0)
    qry_idx = lax.broadcasted_iota(jnp.int32, (tk, cw), 1)
    causal = [key_idx <= qry_idx + c * cw for c in range(tq // cw)]
    nt_dims = (((1,), (1,)), ((), ()))
    head_lanes = lambda h: slice(h * LANES, (h + 1) * LANES)

    def visit(j, j_prev, diagonal):
        keys = pl.ds(pl.multiple_of(j * tk, tk), tk)

        def scores(h):
            s_ref[h] = lax.dot_general(k_ref[keys, head_lanes(h)], q_ref[:, head_lanes(h)], nt_dims,
                                       preferred_element_type=F32)

        for h in range(MXU_LEAD):
            scores(h)
        for h in range(MLA_HEADS):
            if diagonal:
                acc_ref[h] = jnp.zeros((LANES, tq), F32)
            else:
                acc_ref[h] += jnp.dot(v_ref[j_prev, head_lanes(h), :], p_ref[h], preferred_element_type=F32)
            if h + MXU_LEAD < MLA_HEADS:
                scores(h + MXU_LEAD)
        for h in range(MLA_HEADS):
            for c in range(tq // cw):
                cols = slice(c * cw, (c + 1) * cw)
                nk = min(tk, (c + 1) * cw) if diagonal else tk
                s = s_ref[h, :nk, cols]
                if diagonal:
                    s = jnp.where(causal[c][:nk], s, -jnp.inf)
                    m_new = jnp.max(s, axis=0, keepdims=True)
                    if nk < tk:
                        p_ref[h, nk:, cols] = jnp.zeros((tk - nk, cw), BF16)
                else:
                    m_old = m_ref[h, :, cols]
                    m_new = jnp.maximum(m_old, jnp.max(s, axis=0, keepdims=True))
                    acc_ref[h, :, cols] *= jnp.exp2(m_old - m_new)
                p_ref[h, :nk, cols] = jnp.exp2(s - m_new).astype(BF16)
                m_ref[h, :, cols] = m_new

    visit(i, i, diagonal=True)

    def body(j, carry):
        visit(j, jnp.where(j == 0, i, j - 1), diagonal=False)
        return carry

    lax.fori_loop(0, i, body, 0)
    j_last = jnp.where(i == 0, i, i - 1)
    for h in range(MLA_HEADS):
        acc_ref[h] += jnp.dot(v_ref[j_last, head_lanes(h), :], p_ref[h], preferred_element_type=F32)
    for p in range(HEAD_PAIRS):
        outs = [acc_ref[2 * p + e, :MLA_V] / acc_ref[2 * p + e, MLA_V:] for e in range(2)]
        o_ref[:, p * LANES:(p + 1) * LANES] = jnp.concatenate(outs, axis=0).T


def _mla_attention(q, k, v, B, S):
    tq = tk = ATT_TILE
    nq = S // tq
    width = MLA_HEADS * LANES
    return pl.pallas_call(
        _mla_kernel,
        grid=(B, nq),
        in_specs=[
            pl.BlockSpec((tq, width), lambda b, i: (b * nq + i, 0)),
            pl.BlockSpec((S, width), lambda b, i: (b, 0)),
            pl.BlockSpec((S // tk, width, tk), lambda b, i: (b, 0, 0)),
        ],
        out_specs=pl.BlockSpec((tq, MLA_WIDTH), lambda b, i: (b * nq + i, 0)),
        out_shape=jax.ShapeDtypeStruct((B * S, MLA_WIDTH), F32),
        scratch_shapes=[
            pltpu.VMEM((MLA_HEADS, tk, tq), F32),
            pltpu.VMEM((MLA_HEADS, tk, tq), BF16),
            pltpu.VMEM((MLA_HEADS, LANES, tq), F32),
            pltpu.VMEM((MLA_HEADS, 1, tq), F32),
        ],
        compiler_params=_cparams("parallel", "parallel"),
        name="mla_attn",
    )(q, k, v)


ROUTER_ROWS = SUBLANES * (1 + N_GROUPS)


def _split_bf16(a):
    hi = a.astype(BF16)
    return hi, (a - hi.astype(F32)).astype(BF16)


def _route_kernel(osb_ref, omla_ref, x_ref, g_sb_ref, g_mla_ref, wout_ref, fn_ref, wr_hi_ref, wr_lo_ref, br_ref,
                  x1_ref, hext_ref, bucket_ref, rank_ref, counts_ref, carry_ref):
    tm = ROW_TILE

    @pl.when(pl.program_id(0) == 0)
    def _():
        carry_ref[...] = jnp.zeros_like(carry_ref)

    o = jnp.concatenate([_rms(osb_ref[...]) * g_sb_ref[...], _rms(omla_ref[...]) * g_mla_ref[...]], axis=-1)
    x1 = x_ref[...] + jnp.dot(o.astype(BF16), wout_ref[...], preferred_element_type=F32)
    x1_ref[...] = x1
    h = _rms(x1) * fn_ref[...]
    hext_ref[:, :D_MODEL] = h

    h_hi, h_lo = _split_bf16(h)
    logits = (jnp.dot(h_hi, wr_hi_ref[...], preferred_element_type=F32)
              + jnp.dot(h_hi, wr_lo_ref[...], preferred_element_type=F32)
              + jnp.dot(h_lo, wr_hi_ref[...], preferred_element_type=F32)) + br_ref[...]
    lt = logits.T

    rid = lax.broadcasted_iota(jnp.int32, (SUBLANES, tm), 0)
    g_logit = jnp.where(rid < N_GROUPS, lt[:SUBLANES], -jnp.inf)
    g_exp = jnp.exp(g_logit - jnp.max(g_logit, axis=0, keepdims=True))
    p_group = g_exp / jnp.sum(g_exp, axis=0, keepdims=True)
    g_val = jnp.max(p_group, axis=0, keepdims=True)
    g_idx = jnp.min(jnp.where(p_group == g_val, rid, SUBLANES), axis=0, keepdims=True)
    local = lt[SUBLANES * N_GROUPS:SUBLANES * (N_GROUPS + 1)]
    for g in range(N_GROUPS - 2, -1, -1):
        local = jnp.where(g_idx == g, lt[SUBLANES * (g + 1):SUBLANES * (g + 2)], local)
    e_exp = jnp.exp(local - jnp.max(local, axis=0, keepdims=True))
    p_exp = e_exp / jnp.sum(e_exp, axis=0, keepdims=True)
    v1 = jnp.max(p_exp, axis=0, keepdims=True)
    i1 = jnp.min(jnp.where(p_exp == v1, rid, SUBLANES), axis=0, keepdims=True)
    rest = jnp.where(rid == i1, -1.0, p_exp)
    v2 = jnp.max(rest, axis=0, keepdims=True)
    i2 = jnp.min(jnp.where(rest == v2, rid, SUBLANES), axis=0, keepdims=True)
    den = v1 + v2
    w1 = g_val * v1 / den
    w2 = g_val * v2 / den
    first_lower = i1 < i2
    e_lo = jnp.where(first_lower, i1, i2)
    e_hi = jnp.where(first_lower, i2, i1)
    w_lo = jnp.where(first_lower, w1, w2)
    w_hi = jnp.where(first_lower, w2, w1)
    pair = ((e_lo * (2 * EXPERTS_PER_GROUP - 1 - e_lo)) >> 1) + (e_hi - e_lo - 1)
    bucket = g_idx * N_PAIRS + pair
    bucket_ref[0] = bucket

    rid_full = lax.broadcasted_iota(jnp.int32, (LANES, tm), 0)
    w_rows = jnp.where(rid_full == 0, w_lo, jnp.where(rid_full == 1, w_hi, 0.0))
    hext_ref[:, D_MODEL:] = w_rows.T

    onehot = (rid_full == bucket).astype(F32)
    trow = lax.broadcasted_iota(jnp.int32, (tm, tm), 0)
    tcol = lax.broadcasted_iota(jnp.int32, (tm, tm), 1)
    earlier = (trow < tcol).astype(BF16)
    before = jnp.dot(onehot.astype(BF16), earlier, preferred_element_type=F32) + carry_ref[...]
    rank_ref[0] = jnp.sum(onehot * before, axis=0, keepdims=True).astype(jnp.int32)
    carry_ref[...] += jnp.sum(onehot, axis=1, keepdims=True)
    counts_ref[...] = carry_ref[...]


def _route(o_sb, o_mla, x2d, g_sb, g_mla, w_out, fn, wr_hi, wr_lo, br):
    T = x2d.shape[0]
    tm = ROW_TILE
    nt = T // tm
    row = lambda n: pl.BlockSpec((tm, n), lambda i: (i, 0))
    full = lambda a: pl.BlockSpec(a.shape, lambda i: (0,) * a.ndim)
    tok = pl.BlockSpec((1, 1, tm), lambda i: (i, 0, 0))
    ins = [o_sb, o_mla, x2d, g_sb, g_mla, w_out, fn, wr_hi, wr_lo, br]
    return pl.pallas_call(
        _route_kernel,
        grid=(nt,),
        in_specs=[row(SB_WIDTH), row(MLA_WIDTH), row(D_MODEL)] + [full(a) for a in ins[3:]],
        out_specs=[row(D_MODEL), row(EXT_WIDTH), tok, tok, pl.BlockSpec((LANES, 1), lambda i: (0, 0))],
        out_shape=[
            jax.ShapeDtypeStruct((T, D_MODEL), F32),
            jax.ShapeDtypeStruct((T, EXT_WIDTH), F32),
            jax.ShapeDtypeStruct((nt, 1, tm), jnp.int32),
            jax.ShapeDtypeStruct((nt, 1, tm), jnp.int32),
            jax.ShapeDtypeStruct((LANES, 1), F32),
        ],
        scratch_shapes=[pltpu.VMEM((LANES, 1), F32)],
        compiler_params=_cparams("arbitrary"),
        name="route",
    )(*ins)


def _dest_kernel(bucket_ref, rank_ref, offs_ref, dest_ref):
    tm = bucket_ref.shape[-1]
    rid = lax.broadcasted_iota(jnp.int32, (LANES, tm), 0)
    start = jnp.sum(jnp.where(rid == bucket_ref[0], offs_ref[...], 0), axis=0, keepdims=True)
    dest_ref[0] = start + rank_ref[0]


def _dest_rows(bucket, rank, offsets):
    nt, _, tm = bucket.shape
    tok = pl.BlockSpec((1, 1, tm), lambda i: (i, 0, 0))
    return pl.pallas_call(
        _dest_kernel,
        grid=(nt,),
        in_specs=[tok, tok, pl.BlockSpec((LANES, 1), lambda i: (0, 0))],
        out_specs=tok,
        out_shape=jax.ShapeDtypeStruct((nt, 1, tm), jnp.int32),
        compiler_params=_cparams("parallel"),
        name="dest",
    )(bucket, rank, offsets)


INVERSE_STEPS = 16


def _inverse_kernel(dest_ref, src_ref):
    k = pl.program_id(0)
    clear_n = src_ref.shape[0] // INVERSE_STEPS
    put_n = dest_ref.shape[0] // INVERSE_STEPS

    @pl.when(k < INVERSE_STEPS)
    def _():
        def clear(p, carry):
            src_ref[k * clear_n + p] = 0
            return carry
        lax.fori_loop(0, clear_n, clear, 0, unroll=16)

    @pl.when(k >= INVERSE_STEPS)
    def _():
        def put(p, carry):
            t = (k - INVERSE_STEPS) * put_n + p
            src_ref[dest_ref[t]] = t
            return carry
        lax.fori_loop(0, put_n, put, 0, unroll=16)


def _source_rows(dest, n_rows):
    assert n_rows % INVERSE_STEPS == 0 and dest.shape[0] % INVERSE_STEPS == 0
    return pl.pallas_call(
        _inverse_kernel,
        grid=(2 * INVERSE_STEPS,),
        in_specs=[pl.BlockSpec(memory_space=pltpu.SMEM)],
        out_specs=pl.BlockSpec(memory_space=pltpu.SMEM),
        out_shape=jax.ShapeDtypeStruct((n_rows,), jnp.int32),
        compiler_params=_cparams("arbitrary"),
        name="inverse",
    )(dest)


def _moe_kernel(elo_ref, ehi_ref, nt_ref, src_ref, hx_ref, wgu_lo_ref, wd_lo_ref, wgu_hi_ref, wd_hi_ref, ys_ref,
                xbuf_ref, sem):
    del elo_ref, ehi_ref
    tile = MOE_TILE
    i = pl.program_id(0)
    nt = nt_ref[0]

    def start_gather(t, slot):
        for r in range(tile):
            pltpu.make_async_copy(hx_ref.at[pl.ds(src_ref[t * tile + r], 1)], xbuf_ref.at[slot, pl.ds(r, 1)],
                                  sem.at[slot]).start()

    def wait_gather(slot):
        pltpu.make_async_copy(hx_ref.at[pl.ds(0, tile)], xbuf_ref.at[slot], sem.at[slot]).wait()

    @pl.when(i >= nt)
    def _():
        ys_ref[...] = jnp.zeros_like(ys_ref)

    @pl.when(i == 0)
    def _():
        start_gather(0, 0)

    def step(slot):
        wait_gather(slot)
        start_gather(jnp.minimum(i + 1, nt - 1), 1 - slot)
        h = xbuf_ref[slot, :, :D_MODEL].astype(BF16)
        gates = xbuf_ref[slot, :, D_MODEL:]

        def expert(wgu_ref, wd_ref):
            gu = jnp.dot(h, wgu_ref[0], preferred_element_type=F32)
            g, u = gu[:, :D_EXPERT], gu[:, D_EXPERT:]
            hid = (g * jax.nn.sigmoid(g)) * u
            return jnp.dot(hid.astype(BF16), wd_ref[0], preferred_element_type=F32)

        ys_ref[...] = (expert(wgu_lo_ref, wd_lo_ref) * gates[:, 0:1]
                       + expert(wgu_hi_ref, wd_hi_ref) * gates[:, 1:2])

        @pl.when(i == nt - 1)
        def _():
            wait_gather(1 - slot)

    for slot in range(2):
        pl.when((i < nt) & (i % 2 == slot))(functools.partial(step, slot))


def _moe(tile_elo, tile_ehi, n_tiles_used, src, hext, w_gu, w_d):
    n_rows = src.shape[0]
    tile = MOE_TILE
    wspec = lambda shape, which: pl.BlockSpec(
        (1,) + shape, lambda i, elo, ehi, nt, src: ((elo, ehi)[which][i], 0, 0))
    return pl.pallas_call(
        _moe_kernel,
        grid_spec=pltpu.PrefetchScalarGridSpec(
            num_scalar_prefetch=4,
            grid=(n_rows // tile,),
            in_specs=[
                pl.BlockSpec(memory_space=pl.ANY),
                wspec((D_MODEL, 2 * D_EXPERT), 0), wspec((D_EXPERT, D_MODEL), 0),
                wspec((D_MODEL, 2 * D_EXPERT), 1), wspec((D_EXPERT, D_MODEL), 1),
            ],
            out_specs=pl.BlockSpec((tile, D_MODEL), lambda i, elo, ehi, nt, src: (i, 0)),
            scratch_shapes=[pltpu.VMEM((2, tile, EXT_WIDTH), F32), pltpu.SemaphoreType.DMA((2,))],
        ),
        out_shape=jax.ShapeDtypeStruct((n_rows, D_MODEL), F32),
        compiler_params=_cparams("arbitrary"),
        name="moe",
    )(tile_elo, tile_ehi, n_tiles_used, src, hext, w_gu, w_d, w_gu, w_d)


def _final_kernel(dest_ref, ys_ref, x1_ref, fn_ref, o_ref, buf_ref, sem):
    tm = x1_ref.shape[0]
    base = pl.program_id(0) * tm

    def issue(r, _):
        d = dest_ref[base + r]
        pltpu.make_async_copy(ys_ref.at[pl.ds(d, 1)], buf_ref.at[pl.ds(r, 1)], sem).start()
        return _

    lax.fori_loop(0, tm, issue, 0, unroll=8)
    pltpu.make_async_copy(ys_ref.at[pl.ds(0, tm)], buf_ref, sem).wait()
    o_ref[...] = _rms(x1_ref[...] + buf_ref[...]) * fn_ref[...]


def _final(dest, ys, x1, fn):
    T, D = x1.shape
    tm = ROW_TILE
    return pl.pallas_call(
        _final_kernel,
        grid_spec=pltpu.PrefetchScalarGridSpec(
            num_scalar_prefetch=1,
            grid=(T // tm,),
            in_specs=[
                pl.BlockSpec(memory_space=pl.ANY),
                pl.BlockSpec((tm, D), lambda i, d: (i, 0)),
                pl.BlockSpec((1, D), lambda i, d: (0, 0)),
            ],
            out_specs=pl.BlockSpec((tm, D), lambda i, d: (i, 0)),
            scratch_shapes=[pltpu.VMEM((tm, D), F32), pltpu.SemaphoreType.DMA],
        ),
        out_shape=jax.ShapeDtypeStruct((T, D), F32),
        compiler_params=_cparams("arbitrary"),
        name="final",
    )(dest, ys, x1, fn)


def _pair_tables():
    lo, hi = [], []
    for g in range(N_GROUPS):
        for a in range(EXPERTS_PER_GROUP):
            for b in range(a + 1, EXPERTS_PER_GROUP):
                lo.append(g * EXPERTS_PER_GROUP + a)
                hi.append(g * EXPERTS_PER_GROUP + b)
    return np.asarray(lo, np.int32), np.asarray(hi, np.int32)


def _attention_weights(w_in, q_norm, w_uq, kv_norm, w_ukv):
    D = w_in.shape[0]
    c0 = 3 * SB_WIDTH
    zeros = lambda r, c: jnp.zeros((r, c), F32)
    w_kr = jnp.concatenate(
        [zeros(D, MLA_NOPE), w_in[:, c0 + Q_LORA + KV_LORA:], zeros(D, LANES - MLA_NOPE - MLA_ROPE)], axis=1)
    dq = MLA_NOPE + MLA_ROPE
    uq = jnp.concatenate(
        [jnp.concatenate([w_uq[:, h * dq:(h + 1) * dq], zeros(Q_LORA, LANES - dq)], axis=1)
         for h in range(MLA_HEADS)], axis=1)
    dkv = MLA_NOPE + MLA_V
    uk = jnp.concatenate(
        [jnp.concatenate([w_ukv[:, h * dkv:h * dkv + MLA_NOPE], zeros(KV_LORA, LANES - MLA_NOPE)], axis=1)
         for h in range(MLA_HEADS)], axis=1)
    uv = jnp.concatenate(
        [jnp.concatenate([w_ukv[:, h * dkv + MLA_NOPE:(h + 1) * dkv].T, zeros(LANES - MLA_V, KV_LORA)], axis=0)
         for h in range(MLA_HEADS)], axis=0)
    return {
        "sb": w_in[:, :2 * SB_WIDTH].astype(BF16),
        "vsb": w_in[:, 2 * SB_WIDTH:c0].T.astype(BF16),
        "cq": w_in[:, c0:c0 + Q_LORA].astype(BF16),
        "ckv": w_in[:, c0 + Q_LORA:c0 + Q_LORA + KV_LORA].astype(BF16),
        "kr": w_kr.astype(BF16),
        "qn": q_norm[None, :],
        "uq": uq.astype(BF16),
        "kvn": kv_norm[None, :],
        "uk": uk.astype(BF16),
        "uv": uv.astype(BF16),
    }


def _router_weights(w_group, b_group, w_expert, b_expert):
    D = w_group.shape[0]
    pad_g = SUBLANES - N_GROUPS
    pad_e = LANES - SUBLANES - N_EXPERTS
    w = jnp.concatenate([w_group, jnp.zeros((D, pad_g), F32), w_expert, jnp.zeros((D, pad_e), F32)], axis=1)
    b = jnp.concatenate([b_group, jnp.zeros((pad_g,), F32), b_expert, jnp.zeros((pad_e,), F32)])[None, :]
    w_hi, w_lo = _split_bf16(w)
    return w_hi, w_lo, b


def _bucket_layout(counts, n_tiles):
    c = counts[:N_BUCKETS, 0].astype(jnp.int32)
    tiles = (c + MOE_TILE - 1) // MOE_TILE
    tile_end = jnp.cumsum(tiles)
    offsets = (tile_end - tiles) * MOE_TILE
    offsets = jnp.concatenate([offsets, jnp.zeros((LANES - N_BUCKETS,), jnp.int32)])[:, None]
    n_used = tile_end[-1]
    tile_id = jnp.minimum(jnp.arange(n_tiles, dtype=jnp.int32), n_used - 1)
    tile_bucket = jnp.searchsorted(tile_end, tile_id, side="right").astype(jnp.int32)
    pair_lo, pair_hi = _pair_tables()
    return offsets, jnp.asarray(pair_lo)[tile_bucket], jnp.asarray(pair_hi)[tile_bucket], n_used[None]


def kernel(x, positions, attn_norm, w_in, q_norm, w_uq, kv_norm, w_ukv, sb_out_norm, mla_out_norm, w_out,
           ffn_norm, w_group_router, b_group_router, w_expert_router, b_expert_router, w_gate, w_up, w_down,
           final_norm):
    B, S, D = x.shape
    T = B * S
    depth = w_in.shape[0]
    assert D == D_MODEL and T % ROW_TILE == 0 and S % ATT_TILE == 0
    assert depth == 1, "the final norm is fused into the last layer's gather kernel"
    n_sorted_tiles = T // MOE_TILE + N_BUCKETS
    n_sorted_rows = n_sorted_tiles * MOE_TILE

    lane = jnp.arange(LANES)
    invf = (ROPE_BASE ** (-(lane % ROPE_HALF).astype(F32) / ROPE_HALF))[None, :]
    pos2d = positions.reshape(T, 1)
    x2d = x.reshape(T, D)
    for l in range(depth):
        aw = _attention_weights(w_in[l], q_norm[l], w_uq[l], kv_norm[l], w_ukv[l])
        q_sb, k_sb, v_sb, q_m, k_m, v_m = _projections(x2d, pos2d, invf, attn_norm[l][None, :], aw)
        o_sb = _sb_attention(q_sb, k_sb, v_sb, B, S)
        o_mla = _mla_attention(q_m, k_m, v_m, B, S)

        wr_hi, wr_lo, br = _router_weights(w_group_router[l], b_group_router[l], w_expert_router[l], b_expert_router[l])
        x1, hext, bucket, rank, counts = _route(
            o_sb, o_mla, x2d, sb_out_norm[l][None, :], mla_out_norm[l][None, :], w_out[l].astype(BF16),
            ffn_norm[l][None, :], wr_hi, wr_lo, br)
        offsets, tile_elo, tile_ehi, n_used = _bucket_layout(counts, n_sorted_tiles)
        dest = _dest_rows(bucket, rank, offsets).reshape(T)

        src = _source_rows(dest, n_sorted_rows)
        w_gu = jnp.concatenate([w_gate[l], w_up[l]], axis=-1).astype(BF16)
        ys = _moe(tile_elo, tile_ehi, n_used, src, hext, w_gu, w_down[l].astype(BF16))
        x2d = _final(dest, ys, x1, final_norm[None, :])
    return x2d.reshape(B, S, D)
```

```python
import functools

import jax
import jax.numpy as jnp
import numpy as np
from jax import lax
from jax.experimental import pallas as pl
from jax.experimental.pallas import tpu as pltpu

F32 = jnp.float32
BF16 = jnp.bfloat16

D_MODEL = 1024
SB_HEADS = 8
SB_HEAD_DIM = 64
SB_WIDTH = SB_HEADS * SB_HEAD_DIM
MLA_HEADS = 8
MLA_NOPE = 64
MLA_ROPE = 32
MLA_V = 64
MLA_WIDTH = MLA_HEADS * MLA_V
Q_LORA = 256
KV_LORA = 128
ROPE_BASE = 10000.0
N_GROUPS = 4
EXPERTS_PER_GROUP = 8
N_EXPERTS = N_GROUPS * EXPERTS_PER_GROUP
D_EXPERT = 256
EPS = 1e-6
LOG2_E = 1.4426950408889634

LANES = 128
SUBLANES = 8
N_PAIRS = EXPERTS_PER_GROUP * (EXPERTS_PER_GROUP - 1) // 2
N_BUCKETS = N_GROUPS * N_PAIRS
assert N_BUCKETS <= LANES
ROPE_HALF = MLA_ROPE // 2
HEAD_PAIRS = SB_HEADS // 2
assert SB_HEADS == MLA_HEADS and 2 * SB_HEAD_DIM == LANES and 2 * MLA_V == LANES

ROW_TILE = 512
ATT_TILE = 256
ATT_CHUNK = LANES
MXU_LEAD = 3
ROUTE_PARTS = 1
MOE_TILE = 256
EXT_WIDTH = D_MODEL + LANES
SB_UNDERFLOW = 151.0
VMEM_LIMIT = 56 * 1024 * 1024


def _rms(x):
    return x * lax.rsqrt(jnp.mean(x * x, axis=-1, keepdims=True) + EPS)


def _cparams(*sem):
    return pltpu.CompilerParams(dimension_semantics=sem, vmem_limit_bytes=VMEM_LIMIT)


def _proj_kernel(x_ref, pos_ref, invf_ref, an_ref, wsb_ref, wvsb_ref, wcq_ref, wckv_ref, wkr_ref, qn_ref, wuq_ref,
                 kvn_ref, wuk_ref, wuv_ref, qsb_ref, ksb_ref, vsb_ref, qm_ref, km_ref, vm_ref):
    tk = ATT_TILE
    nt_dims = (((1,), (1,)), ((), ()))
    hb = (_rms(x_ref[...]) * an_ref[...]).astype(BF16)
    cq = jnp.dot(hb, wcq_ref[...], preferred_element_type=F32)
    ckv = jnp.dot(hb, wckv_ref[...], preferred_element_type=F32)
    kr = jnp.dot(hb, wkr_ref[...], preferred_element_type=F32)
    sb = jnp.dot(hb, wsb_ref[...], preferred_element_type=F32)
    qsb_ref[...] = (sb[:, :SB_WIDTH] * (SB_HEAD_DIM ** -0.5 * LOG2_E)).astype(BF16)
    ksb_ref[...] = sb[:, SB_WIDTH:].astype(BF16)
    v_sb = lax.dot_general(wvsb_ref[...], hb, nt_dims, preferred_element_type=F32).astype(BF16)
    for kb in range(v_sb.shape[1] // tk):
        vsb_ref[kb] = v_sb[:, kb * tk:(kb + 1) * tk]

    q = jnp.dot((_rms(cq) * qn_ref[...]).astype(BF16), wuq_ref[...], preferred_element_type=F32)
    ckn = (_rms(ckv) * kvn_ref[...]).astype(BF16)
    kn = jnp.dot(ckn, wuk_ref[...], preferred_element_type=F32)
    v_m = lax.dot_general(wuv_ref[...], ckn, nt_dims, preferred_element_type=F32)
    ones_rows = (lax.broadcasted_iota(jnp.int32, (MLA_HEADS * LANES, 1), 0) % LANES) >= MLA_V
    v_m = jnp.where(ones_rows, 1.0, v_m).astype(BF16)
    for kb in range(v_m.shape[1] // tk):
        vm_ref[kb] = v_m[:, kb * tk:(kb + 1) * tk]

    ang = pos_ref[...].astype(F32) * invf_ref[...]
    cos, sin = jnp.cos(ang), jnp.sin(ang)
    lane = lax.broadcasted_iota(jnp.int32, (1, LANES), 1)
    x1_lanes = (lane >= MLA_NOPE) & (lane < MLA_NOPE + ROPE_HALF)
    x2_lanes = (lane >= MLA_NOPE + ROPE_HALF) & (lane < MLA_NOPE + MLA_ROPE)
    c_tab = jnp.where(lane < MLA_NOPE, 1.0, jnp.where(x1_lanes | x2_lanes, cos, 0.0))
    s_from_x2 = jnp.where(x1_lanes, -sin, 0.0)
    s_from_x1 = jnp.where(x2_lanes, sin, 0.0)

    def rope(t):
        return (t * c_tab + pltpu.roll(t, LANES - ROPE_HALF, 1) * s_from_x2
                + pltpu.roll(t, ROPE_HALF, 1) * s_from_x1)

    k_rope = rope(kr)
    q_scale = (MLA_NOPE + MLA_ROPE) ** -0.5 * LOG2_E
    for h in range(MLA_HEADS):
        blk = slice(h * LANES, (h + 1) * LANES)
        qm_ref[:, blk] = (rope(q[:, blk]) * q_scale).astype(BF16)
        km_ref[:, blk] = (kn[:, blk] + k_rope).astype(BF16)


def _projections(x2d, pos2d, invf, an, w):
    T = x2d.shape[0]
    tm = ROW_TILE
    tk = ATT_TILE
    row = lambda n: pl.BlockSpec((tm, n), lambda i: (i, 0))
    slab = lambda n: pl.BlockSpec((tm // tk, n, tk), lambda i: (i, 0, 0))
    full = lambda a: pl.BlockSpec(a.shape, lambda i: (0,) * a.ndim)
    ins = [x2d, pos2d, invf, an, w["sb"], w["vsb"], w["cq"], w["ckv"], w["kr"], w["qn"], w["uq"], w["kvn"],
           w["uk"], w["uv"]]
    in_specs = [row(D_MODEL), row(1)] + [full(a) for a in ins[2:]]
    wide = MLA_HEADS * LANES
    row_out = lambda n: jax.ShapeDtypeStruct((T, n), BF16)
    slab_out = lambda n: jax.ShapeDtypeStruct((T // tk, n, tk), BF16)
    return pl.pallas_call(
        _proj_kernel,
        grid=(T // tm,),
        in_specs=in_specs,
        out_specs=[row(SB_WIDTH), row(SB_WIDTH), slab(SB_WIDTH), row(wide), row(wide), slab(wide)],
        out_shape=[row_out(SB_WIDTH), row_out(SB_WIDTH), slab_out(SB_WIDTH), row_out(wide), row_out(wide),
                   slab_out(wide)],
        compiler_params=_cparams("parallel"),
        name="proj",
    )(*ins)


def _softplus2(z):
    sign_bit = jnp.uint32(0x80000000)
    neg_abs = lax.bitcast_convert_type(lax.bitcast_convert_type(z, jnp.uint32) | sign_bit, F32)
    return jnp.maximum(z, 0.0) + jnp.log2(1.0 + jnp.exp2(neg_abs))


def _sb_kernel(q_ref, k_ref, v_ref, o_ref, z_ref, w_ref, acc_ref, c_ref, scale_ref):
    tq = tk = ATT_TILE
    cw = ATT_CHUNK
    i = pl.program_id(1)
    lane = lax.broadcasted_iota(jnp.int32, (1, LANES), 1)
    in_head = (lane < SB_HEAD_DIM, lane >= SB_HEAD_DIM)
    later = (lax.broadcasted_iota(jnp.int32, (tk, tk), 1)
             > lax.broadcasted_iota(jnp.int32, (tk, tk), 0)).astype(BF16)
    key_idx = lax.broadcasted_iota(jnp.int32, (tk, cw), 0)
    qry_idx = lax.broadcasted_iota(jnp.int32, (tk, cw), 1)
    strict = [key_idx < qry_idx + c * cw for c in range(tq // cw)]
    nt_dims = (((1,), (1,)), ((), ()))
    pair_lanes = lambda h: slice((h // 2) * LANES, (h // 2 + 1) * LANES)

    def add_values(j_blk, h):
        v_head = v_ref[j_blk, h * SB_HEAD_DIM:(h + 1) * SB_HEAD_DIM, :]
        acc_ref[h] += jnp.dot(v_head, w_ref[h], preferred_element_type=F32) * scale_ref[h]

    def visit(j, j_prev, diagonal):
        keys = pl.ds(pl.multiple_of(j * tk, tk), tk)
        for h in range(SB_HEADS):
            q_pair = q_ref[:, pair_lanes(h)]
            qh = jnp.where(in_head[h % 2], q_pair, jnp.zeros_like(q_pair))
            z_ref[h] = lax.dot_general(k_ref[keys, pair_lanes(h)], qh, nt_dims, preferred_element_type=F32)
            if diagonal:
                acc_ref[h] = jnp.zeros((SB_HEAD_DIM, tq), F32)
            else:
                add_values(j_prev, h)
        for h in range(SB_HEADS):
            for c in range(tq // cw):
                cols = slice(c * cw, (c + 1) * cw)
                nk = min(tk, (c + 1) * cw) if diagonal else tk
                z = z_ref[h, :nk, cols]
                sp = _softplus2(z)
                if diagonal:
                    sp = jnp.where(strict[c][:nk], sp, 0.0)
                    if nk < tk:
                        w_ref[h, nk:, cols] = jnp.zeros((tk - nk, cw), BF16)
                z_ref[h, :nk, cols] = z - sp
                w_ref[h, :nk, cols] = sp.astype(BF16)
        c_low = None
        for h in range(SB_HEADS):
            suffix = jnp.dot(later, w_ref[h], preferred_element_type=F32)
            block_sum = suffix[0:1, :] + w_ref[h, 0:1, :].astype(F32)
            if diagonal:
                scale_ref[h] = jnp.ones((1, tq), F32)
                c_new = block_sum
            else:
                c_old = c_ref[h]
                scale_ref[h] = jnp.exp2(-c_old)
                c_new = c_old + block_sum
            c_ref[h] = c_new
            c_low = c_new if c_low is None else jnp.minimum(c_low, c_new)
            for c in range(tq // cw):
                cols = slice(c * cw, (c + 1) * cw)
                nk = min(tk, (c + 1) * cw) if diagonal else tk
                a = jnp.exp2(z_ref[h, :nk, cols] - suffix[:nk, cols])
                if diagonal:
                    a = jnp.where(strict[c][:nk], a, 0.0)
                w_ref[h, :nk, cols] = a.astype(BF16)
        return jnp.min(c_low)

    def cond(carry):
        j, _, c_min = carry
        return (j >= 0) & (c_min < SB_UNDERFLOW)

    def body(carry):
        j, j_prev, _ = carry
        return j - 1, j, visit(j, j_prev, diagonal=False)

    _, j_last, _ = lax.while_loop(cond, body, (i - 1, i, visit(i, i, diagonal=True)))
    for h in range(SB_HEADS):
        add_values(j_last, h)
    for p in range(HEAD_PAIRS):
        o_ref[:, p * LANES:(p + 1) * LANES] = jnp.concatenate([acc_ref[2 * p], acc_ref[2 * p + 1]], axis=0).T


def _sb_attention(q, k, v, B, S):
    tq = tk = ATT_TILE
    nq = S // tq
    return pl.pallas_call(
        _sb_kernel,
        grid=(B, nq),
        in_specs=[
            pl.BlockSpec((tq, SB_WIDTH), lambda b, i: (b * nq + i, 0)),
            pl.BlockSpec((S, SB_WIDTH), lambda b, i: (b, 0)),
            pl.BlockSpec((S // tk, SB_WIDTH, tk), lambda b, i: (b, 0, 0)),
        ],
        out_specs=pl.BlockSpec((tq, SB_WIDTH), lambda b, i: (b * nq + i, 0)),
        out_shape=jax.ShapeDtypeStruct((B * S, SB_WIDTH), F32),
        scratch_shapes=[
            pltpu.VMEM((SB_HEADS, tk, tq), F32),
            pltpu.VMEM((SB_HEADS, tk, tq), BF16),
            pltpu.VMEM((SB_HEADS, SB_HEAD_DIM, tq), F32),
            pltpu.VMEM((SB_HEADS, 1, tq), F32),
            pltpu.VMEM((SB_HEADS, 1, tq), F32),
        ],
        compiler_params=_cparams("parallel", "parallel"),
        name="sb_attn",
    )(q, k, v)


def _mla_kernel(q_ref, k_ref, v_ref, o_ref, s_ref, p_ref, acc_ref, m_ref):
    tq = tk = ATT_TILE
    cw = ATT_CHUNK
    i = pl.program_id(1)
    key_idx = lax.broadcasted_iota(jnp.int32, (tk, cw), 0)
    qry_idx = lax.broadcasted_iota(jnp.int32, (tk, cw), 1)
    causal = [key_idx <= qry_idx + c * cw for c in range(tq // cw)]
    nt_dims = (((1,), (1,)), ((), ()))
    head_lanes = lambda h: slice(h * LANES, (h + 1) * LANES)

    def visit(j, j_prev, diagonal):
        keys = pl.ds(pl.multiple_of(j * tk, tk), tk)

        def scores(h):
            s_ref[h] = lax.dot_general(k_ref[keys, head_lanes(h)], q_ref[:, head_lanes(h)], nt_dims,
                                       preferred_element_type=F32)

        for h in range(MXU_LEAD):
            scores(h)
        for h in range(MLA_HEADS):
            if diagonal:
                acc_ref[h] = jnp.zeros((LANES, tq), F32)
            else:
                acc_ref[h] += jnp.dot(v_ref[j_prev, head_lanes(h), :], p_ref[h], preferred_element_type=F32)
            if h + MXU_LEAD < MLA_HEADS:
                scores(h + MXU_LEAD)
        for h in range(MLA_HEADS):
            for c in range(tq // cw):
                cols = slice(c * cw, (c + 1) * cw)
                nk = min(tk, (c + 1) * cw) if diagonal else tk
                s = s_ref[h, :nk, cols]
                if diagonal:
                    s = jnp.where(causal[c][:nk], s, -jnp.inf)
                    m_new = jnp.max(s, axis=0, keepdims=True)
                    if nk < tk:
                        p_ref[h, nk:, cols] = jnp.zeros((tk - nk, cw), BF16)
                else:
                    m_old = m_ref[h, :, cols]
                    m_new = jnp.maximum(m_old, jnp.max(s, axis=0, keepdims=True))
                    acc_ref[h, :, cols] *= jnp.exp2(m_old - m_new)
                p_ref[h, :nk, cols] = jnp.exp2(s - m_new).astype(BF16)
                m_ref[h, :, cols] = m_new

    visit(i, i, diagonal=True)

    def body(j, carry):
        visit(j, jnp.where(j == 0, i, j - 1), diagonal=False)
        return carry

    lax.fori_loop(0, i, body, 0)
    j_last = jnp.where(i == 0, i, i - 1)
    for h in range(MLA_HEADS):
        acc_ref[h] += jnp.dot(v_ref[j_last, head_lanes(h), :], p_ref[h], preferred_element_type=F32)
    for p in range(HEAD_PAIRS):
        outs = [acc_ref[2 * p + e, :MLA_V] / acc_ref[2 * p + e, MLA_V:] for e in range(2)]
        o_ref[:, p * LANES:(p + 1) * LANES] = jnp.concatenate(outs, axis=0).T


def _mla_attention(q, k, v, B, S):
    tq = tk = ATT_TILE
    nq = S // tq
    width = MLA_HEADS * LANES
    return pl.pallas_call(
        _mla_kernel,
        grid=(B, nq),
        in_specs=[
            pl.BlockSpec((tq, width), lambda b, i: (b * nq + i, 0)),
            pl.BlockSpec((S, width), lambda b, i: (b, 0)),
            pl.BlockSpec((S // tk, width, tk), lambda b, i: (b, 0, 0)),
        ],
        out_specs=pl.BlockSpec((tq, MLA_WIDTH), lambda b, i: (b * nq + i, 0)),
        out_shape=jax.ShapeDtypeStruct((B * S, MLA_WIDTH), F32),
        scratch_shapes=[
            pltpu.VMEM((MLA_HEADS, tk, tq), F32),
            pltpu.VMEM((MLA_HEADS, tk, tq), BF16),
            pltpu.VMEM((MLA_HEADS, LANES, tq), F32),
            pltpu.VMEM((MLA_HEADS, 1, tq), F32),
        ],
        compiler_params=_cparams("parallel", "parallel"),
        name="mla_attn",
    )(q, k, v)


ROUTER_ROWS = SUBLANES * (1 + N_GROUPS)


def _split_bf16(a):
    hi = a.astype(BF16)
    return hi, (a - hi.astype(F32)).astype(BF16)


def _route_kernel(osb_ref, omla_ref, x_ref, g_sb_ref, g_mla_ref, wout_ref, fn_ref, wr_ref, br_ref,
                  x1_ref, hext_ref, bucket_ref, rank_ref, counts_ref, carry_ref):
    tm = ROW_TILE

    @pl.when(pl.program_id(0) == 0)
    def _():
        carry_ref[...] = jnp.zeros_like(carry_ref)

    part = tm // ROUTE_PARTS
    parts = [slice(p * part, (p + 1) * part) for p in range(ROUTE_PARTS)]
    x1s = []
    for rows in parts:
        o = jnp.concatenate([_rms(osb_ref[rows]) * g_sb_ref[...], _rms(omla_ref[rows]) * g_mla_ref[...]], axis=-1)
        x1 = x_ref[rows] + jnp.dot(o.astype(BF16), wout_ref[...], preferred_element_type=F32)
        x1_ref[rows] = x1
        x1s.append(x1)
    logit_parts = []
    for rows, x1 in zip(parts, x1s):
        h = _rms(x1) * fn_ref[...]
        hext_ref[rows, :D_MODEL] = h
        h_hi, h_lo = _split_bf16(h)
        both = jnp.dot(h_hi, wr_ref[...], preferred_element_type=F32)
        logit_parts.append((both[:, :LANES] + both[:, LANES:]
                            + jnp.dot(h_lo, wr_ref[:, :LANES], preferred_element_type=F32)) + br_ref[...])
    for p, (rows, logits) in enumerate(zip(parts, logit_parts)):
        _route_part(logits, rows, slice(p * part, (p + 1) * part), hext_ref, bucket_ref, rank_ref, carry_ref)
    counts_ref[...] = carry_ref[...]


def _route_part(logits, rows, cols, hext_ref, bucket_ref, rank_ref, carry_ref):
    tm = logits.shape[0]
    lt = logits.T

    rid = lax.broadcasted_iota(jnp.int32, (SUBLANES, tm), 0)
    g_logit = jnp.where(rid < N_GROUPS, lt[:SUBLANES], -jnp.inf)
    g_exp = jnp.exp(g_logit - jnp.max(g_logit, axis=0, keepdims=True))
    p_group = g_exp / jnp.sum(g_exp, axis=0, keepdims=True)
    g_val = jnp.max(p_group, axis=0, keepdims=True)
    g_idx = jnp.min(jnp.where(p_group == g_val, rid, SUBLANES), axis=0, keepdims=True)
    local = lt[SUBLANES * N_GROUPS:SUBLANES * (N_GROUPS + 1)]
    for g in range(N_GROUPS - 2, -1, -1):
        local = jnp.where(g_idx == g, lt[SUBLANES * (g + 1):SUBLANES * (g + 2)], local)
    e_exp = jnp.exp(local - jnp.max(local, axis=0, keepdims=True))
    p_exp = e_exp / jnp.sum(e_exp, axis=0, keepdims=True)
    v1 = jnp.max(p_exp, axis=0, keepdims=True)
    i1 = jnp.min(jnp.where(p_exp == v1, rid, SUBLANES), axis=0, keepdims=True)
    rest = jnp.where(rid == i1, -1.0, p_exp)
    v2 = jnp.max(rest, axis=0, keepdims=True)
    i2 = jnp.min(jnp.where(rest == v2, rid, SUBLANES), axis=0, keepdims=True)
    den = v1 + v2
    w1 = g_val * v1 / den
    w2 = g_val * v2 / den
    first_lower = i1 < i2
    e_lo = jnp.where(first_lower, i1, i2)
    e_hi = jnp.where(first_lower, i2, i1)
    w_lo = jnp.where(first_lower, w1, w2)
    w_hi = jnp.where(first_lower, w2, w1)
    pair = ((e_lo * (2 * EXPERTS_PER_GROUP - 1 - e_lo)) >> 1) + (e_hi - e_lo - 1)
    bucket = g_idx * N_PAIRS + pair
    bucket_ref[0, :, cols] = bucket

    rid_full = lax.broadcasted_iota(jnp.int32, (LANES, tm), 0)
    w_rows = jnp.where(rid_full == 0, w_lo, jnp.where(rid_full == 1, w_hi, 0.0))
    hext_ref[rows, D_MODEL:] = w_rows.T

    onehot = (rid_full == bucket).astype(F32)
    trow = lax.broadcasted_iota(jnp.int32, (tm, tm), 0)
    tcol = lax.broadcasted_iota(jnp.int32, (tm, tm), 1)
    earlier = (trow < tcol).astype(BF16)
    before = jnp.dot(onehot.astype(BF16), earlier, preferred_element_type=F32) + carry_ref[...]
    rank_ref[0, :, cols] = jnp.sum(onehot * before, axis=0, keepdims=True).astype(jnp.int32)
    carry_ref[...] += jnp.sum(onehot, axis=1, keepdims=True)


def _route(o_sb, o_mla, x2d, g_sb, g_mla, w_out, fn, wr, br):
    T = x2d.shape[0]
    tm = ROW_TILE
    nt = T // tm
    row = lambda n: pl.BlockSpec((tm, n), lambda i: (i, 0))
    full = lambda a: pl.BlockSpec(a.shape, lambda i: (0,) * a.ndim)
    tok = pl.BlockSpec((1, 1, tm), lambda i: (i, 0, 0))
    ins = [o_sb, o_mla, x2d, g_sb, g_mla, w_out, fn, wr, br]
    return pl.pallas_call(
        _route_kernel,
        grid=(nt,),
        in_specs=[row(SB_WIDTH), row(MLA_WIDTH), row(D_MODEL)] + [full(a) for a in ins[3:]],
        out_specs=[row(D_MODEL), row(EXT_WIDTH), tok, tok, pl.BlockSpec((LANES, 1), lambda i: (0, 0))],
        out_shape=[
            jax.ShapeDtypeStruct((T, D_MODEL), F32),
            jax.ShapeDtypeStruct((T, EXT_WIDTH), F32),
            jax.ShapeDtypeStruct((nt, 1, tm), jnp.int32),
            jax.ShapeDtypeStruct((nt, 1, tm), jnp.int32),
            jax.ShapeDtypeStruct((LANES, 1), F32),
        ],
        scratch_shapes=[pltpu.VMEM((LANES, 1), F32)],
        compiler_params=_cparams("arbitrary"),
        name="route",
    )(*ins)


DEST_TILES = 8


def _dest_kernel(bucket_ref, rank_ref, offs_ref, dest_ref):
    tm = bucket_ref.shape[-1]
    rid = lax.broadcasted_iota(jnp.int32, (LANES, tm), 0)
    for t in range(bucket_ref.shape[0]):
        start = jnp.sum(jnp.where(rid == bucket_ref[t], offs_ref[...], 0), axis=0, keepdims=True)
        dest_ref[t] = start + rank_ref[t]


def _dest_rows(bucket, rank, offsets):
    nt, _, tm = bucket.shape
    assert nt % DEST_TILES == 0
    tok = pl.BlockSpec((DEST_TILES, 1, tm), lambda i: (i, 0, 0))
    return pl.pallas_call(
        _dest_kernel,
        grid=(nt // DEST_TILES,),
        in_specs=[tok, tok, pl.BlockSpec((LANES, 1), lambda i: (0, 0))],
        out_specs=tok,
        out_shape=jax.ShapeDtypeStruct((nt, 1, tm), jnp.int32),
        compiler_params=_cparams("parallel"),
        name="dest",
    )(bucket, rank, offsets)


def _scatter_kernel(dest_ref, h_ref, init_ref, xs_ref, sem):
    del init_ref
    tm = h_ref.shape[0]
    base = pl.program_id(0) * tm

    def issue(r, carry):
        d = dest_ref[base + r]
        pltpu.make_async_copy(h_ref.at[pl.ds(r, 1)], xs_ref.at[pl.ds(d, 1)], sem).start()
        return carry

    lax.fori_loop(0, tm, issue, 0, unroll=8)
    pltpu.make_async_copy(h_ref, xs_ref.at[pl.ds(0, tm)], sem).wait()


def _scatter_rows(dest, hext, n_rows):
    T, W = hext.shape
    tm = ROW_TILE
    init = jnp.zeros((n_rows, W), hext.dtype)
    return pl.pallas_call(
        _scatter_kernel,
        grid_spec=pltpu.PrefetchScalarGridSpec(
            num_scalar_prefetch=1,
            grid=(T // tm,),
            in_specs=[pl.BlockSpec((tm, W), lambda i, d: (i, 0)), pl.BlockSpec(memory_space=pl.ANY)],
            out_specs=pl.BlockSpec(memory_space=pl.ANY),
            scratch_shapes=[pltpu.SemaphoreType.DMA],
        ),
        out_shape=jax.ShapeDtypeStruct((n_rows, W), hext.dtype),
        input_output_aliases={2: 0},
        compiler_params=_cparams("arbitrary"),
        name="scatter",
    )(dest, hext, init)


def _moe_kernel(elo_ref, ehi_ref, nt_ref, xs_ref, wgu_lo_ref, wd_lo_ref, wgu_hi_ref, wd_hi_ref, ys_ref):
    del elo_ref, ehi_ref
    used = pl.program_id(0) < nt_ref[0]

    @pl.when(jnp.logical_not(used))
    def _():
        ys_ref[...] = jnp.zeros_like(ys_ref)

    @pl.when(used)
    def _():
        h = xs_ref[:, :D_MODEL].astype(BF16)
        gates = xs_ref[:, D_MODEL:]
        gu_lo = jnp.dot(h, wgu_lo_ref[0], preferred_element_type=F32)
        gu_hi = jnp.dot(h, wgu_hi_ref[0], preferred_element_type=F32)

        def down(gu, wd_ref):
            g, u = gu[:, :D_EXPERT], gu[:, D_EXPERT:]
            hid = (g * jax.nn.sigmoid(g)) * u
            return jnp.dot(hid.astype(BF16), wd_ref[0], preferred_element_type=F32)

        ys_ref[...] = down(gu_lo, wd_lo_ref) * gates[:, 0:1] + down(gu_hi, wd_hi_ref) * gates[:, 1:2]


def _moe(tile_elo, tile_ehi, n_tiles_used, xs, w_gu, w_d):
    n_rows = xs.shape[0]
    tile = MOE_TILE
    rows = lambda w: pl.BlockSpec((tile, w), lambda i, elo, ehi, nt: (jnp.minimum(i, nt[0] - 1), 0))
    wspec = lambda shape, which: pl.BlockSpec(
        (1,) + shape, lambda i, elo, ehi, nt: ((elo, ehi)[which][i], 0, 0))
    return pl.pallas_call(
        _moe_kernel,
        grid_spec=pltpu.PrefetchScalarGridSpec(
            num_scalar_prefetch=3,
            grid=(n_rows // tile,),
            in_specs=[
                rows(EXT_WIDTH),
                wspec((D_MODEL, 2 * D_EXPERT), 0), wspec((D_EXPERT, D_MODEL), 0),
                wspec((D_MODEL, 2 * D_EXPERT), 1), wspec((D_EXPERT, D_MODEL), 1),
            ],
            out_specs=pl.BlockSpec((tile, D_MODEL), lambda i, elo, ehi, nt: (i, 0)),
        ),
        out_shape=jax.ShapeDtypeStruct((n_rows, D_MODEL), F32),
        compiler_params=_cparams("arbitrary"),
        name="moe",
    )(tile_elo, tile_ehi, n_tiles_used, xs, w_gu, w_d, w_gu, w_d)


def _final_kernel(dest_ref, ys_ref, x1_ref, fn_ref, o_ref, buf_ref, sem):
    tm = x1_ref.shape[0]
    i = pl.program_id(0)
    slot = i % 2

    def start_gather(t, s):
        def issue(r, carry):
            d = dest_ref[t * tm + r]
            pltpu.make_async_copy(ys_ref.at[pl.ds(d, 1)], buf_ref.at[s, pl.ds(r, 1)], sem.at[s]).start()
            return carry
        lax.fori_loop(0, tm, issue, 0, unroll=8)

    @pl.when(i == 0)
    def _():
        start_gather(0, 0)

    @pl.when(i + 1 < pl.num_programs(0))
    def _():
        start_gather(i + 1, 1 - slot)

    pltpu.make_async_copy(ys_ref.at[pl.ds(0, tm)], buf_ref.at[slot], sem.at[slot]).wait()
    o_ref[...] = _rms(x1_ref[...] + buf_ref[slot]) * fn_ref[...]


def _final(dest, ys, x1, fn):
    T, D = x1.shape
    tm = ROW_TILE
    return pl.pallas_call(
        _final_kernel,
        grid_spec=pltpu.PrefetchScalarGridSpec(
            num_scalar_prefetch=1,
            grid=(T // tm,),
            in_specs=[
                pl.BlockSpec(memory_space=pl.ANY),
                pl.BlockSpec((tm, D), lambda i, d: (i, 0)),
                pl.BlockSpec((1, D), lambda i, d: (0, 0)),
            ],
            out_specs=pl.BlockSpec((tm, D), lambda i, d: (i, 0)),
            scratch_shapes=[pltpu.VMEM((2, tm, D), F32), pltpu.SemaphoreType.DMA((2,))],
        ),
        out_shape=jax.ShapeDtypeStruct((T, D), F32),
        compiler_params=_cparams("arbitrary"),
        name="final",
    )(dest, ys, x1, fn)


def _pair_tables():
    lo, hi = [], []
    for g in range(N_GROUPS):
        for a in range(EXPERTS_PER_GROUP):
            for b in range(a + 1, EXPERTS_PER_GROUP):
                lo.append(g * EXPERTS_PER_GROUP + a)
                hi.append(g * EXPERTS_PER_GROUP + b)
    return np.asarray(lo, np.int32), np.asarray(hi, np.int32)


def _attention_weights(w_in, q_norm, w_uq, kv_norm, w_ukv):
    D = w_in.shape[0]
    c0 = 3 * SB_WIDTH
    zeros = lambda r, c: jnp.zeros((r, c), F32)
    w_kr = jnp.concatenate(
        [zeros(D, MLA_NOPE), w_in[:, c0 + Q_LORA + KV_LORA:], zeros(D, LANES - MLA_NOPE - MLA_ROPE)], axis=1)
    dq = MLA_NOPE + MLA_ROPE
    uq = jnp.concatenate(
        [jnp.concatenate([w_uq[:, h * dq:(h + 1) * dq], zeros(Q_LORA, LANES - dq)], axis=1)
         for h in range(MLA_HEADS)], axis=1)
    dkv = MLA_NOPE + MLA_V
    uk = jnp.concatenate(
        [jnp.concatenate([w_ukv[:, h * dkv:h * dkv + MLA_NOPE], zeros(KV_LORA, LANES - MLA_NOPE)], axis=1)
         for h in range(MLA_HEADS)], axis=1)
    uv = jnp.concatenate(
        [jnp.concatenate([w_ukv[:, h * dkv + MLA_NOPE:(h + 1) * dkv].T, zeros(LANES - MLA_V, KV_LORA)], axis=0)
         for h in range(MLA_HEADS)], axis=0)
    return {
        "sb": w_in[:, :2 * SB_WIDTH].astype(BF16),
        "vsb": w_in[:, 2 * SB_WIDTH:c0].T.astype(BF16),
        "cq": w_in[:, c0:c0 + Q_LORA].astype(BF16),
        "ckv": w_in[:, c0 + Q_LORA:c0 + Q_LORA + KV_LORA].astype(BF16),
        "kr": w_kr.astype(BF16),
        "qn": q_norm[None, :],
        "uq": uq.astype(BF16),
        "kvn": kv_norm[None, :],
        "uk": uk.astype(BF16),
        "uv": uv.astype(BF16),
    }


def _router_weights(w_group, b_group, w_expert, b_expert):
    D = w_group.shape[0]
    pad_g = SUBLANES - N_GROUPS
    pad_e = LANES - SUBLANES - N_EXPERTS
    w = jnp.concatenate([w_group, jnp.zeros((D, pad_g), F32), w_expert, jnp.zeros((D, pad_e), F32)], axis=1)
    b = jnp.concatenate([b_group, jnp.zeros((pad_g,), F32), b_expert, jnp.zeros((pad_e,), F32)])[None, :]
    return jnp.concatenate(_split_bf16(w), axis=1), b


def _bucket_layout(counts, n_tiles):
    c = counts[:N_BUCKETS, 0].astype(jnp.int32)
    tiles = (c + MOE_TILE - 1) // MOE_TILE
    tile_end = jnp.cumsum(tiles)
    starts = (tile_end - tiles) * MOE_TILE
    offsets = jnp.concatenate([starts, jnp.zeros((LANES - N_BUCKETS,), jnp.int32)])[:, None]
    n_used = tile_end[-1]
    tile_id = jnp.minimum(jnp.arange(n_tiles, dtype=jnp.int32), n_used - 1)
    tile_bucket = jnp.sum(tile_id[:, None] >= tile_end[None, :], axis=1)
    onehot = tile_bucket[:, None] == jnp.arange(N_BUCKETS)[None, :]
    pick = lambda table: jnp.sum(jnp.where(onehot, table[None, :], 0), axis=1).astype(jnp.int32)
    pair_lo, pair_hi = _pair_tables()
    return offsets, pick(jnp.asarray(pair_lo)), pick(jnp.asarray(pair_hi)), n_used[None]


def kernel(x, positions, attn_norm, w_in, q_norm, w_uq, kv_norm, w_ukv, sb_out_norm, mla_out_norm, w_out,
           ffn_norm, w_group_router, b_group_router, w_expert_router, b_expert_router, w_gate, w_up, w_down,
           final_norm):
    B, S, D = x.shape
    T = B * S
    depth = w_in.shape[0]
    assert D == D_MODEL and T % ROW_TILE == 0 and S % ATT_TILE == 0
    assert depth == 1, "the final norm is fused into the last layer's gather kernel"
    n_sorted_tiles = T // MOE_TILE + N_BUCKETS
    n_sorted_rows = n_sorted_tiles * MOE_TILE

    lane = jnp.arange(LANES)
    invf = (ROPE_BASE ** (-(lane % ROPE_HALF).astype(F32) / ROPE_HALF))[None, :]
    pos2d = positions.reshape(T, 1)
    x2d = x.reshape(T, D)
    for l in range(depth):
        aw = _attention_weights(w_in[l], q_norm[l], w_uq[l], kv_norm[l], w_ukv[l])
        q_sb, k_sb, v_sb, q_m, k_m, v_m = _projections(x2d, pos2d, invf, attn_norm[l][None, :], aw)
        o_sb = _sb_attention(q_sb, k_sb, v_sb, B, S)
        o_mla = _mla_attention(q_m, k_m, v_m, B, S)

        wr, br = _router_weights(w_group_router[l], b_group_router[l], w_expert_router[l], b_expert_router[l])
        x1, hext, bucket, rank, counts = _route(
            o_sb, o_mla, x2d, sb_out_norm[l][None, :], mla_out_norm[l][None, :], w_out[l].astype(BF16),
            ffn_norm[l][None, :], wr, br)
        offsets, tile_elo, tile_ehi, n_used = _bucket_layout(counts, n_sorted_tiles)
        dest = _dest_rows(bucket, rank, offsets).reshape(T)

        xs = _scatter_rows(dest, hext, n_sorted_rows)
        w_gu = jnp.concatenate([w_gate[l], w_up[l]], axis=-1).astype(BF16)
        ys = _moe(tile_elo, tile_ehi, n_used, xs, w_gu, w_down[l].astype(BF16))
        x2d = _final(dest, ys, x1, final_norm[None, :])
    return x2d.reshape(B, S, D)
```

```python
import functools

import jax
import jax.numpy as jnp
import numpy as np
from jax import lax
from jax.experimental import pallas as pl
from jax.experimental.pallas import tpu as pltpu

F32 = jnp.float32
BF16 = jnp.bfloat16

D_MODEL = 1024
SB_HEADS = 8
SB_HEAD_DIM = 64
SB_WIDTH = SB_HEADS * SB_HEAD_DIM
MLA_HEADS = 8
MLA_NOPE = 64
MLA_ROPE = 32
MLA_V = 64
MLA_WIDTH = MLA_HEADS * MLA_V
Q_LORA = 256
KV_LORA = 128
ROPE_BASE = 10000.0
N_GROUPS = 4
EXPERTS_PER_GROUP = 8
N_EXPERTS = N_GROUPS * EXPERTS_PER_GROUP
D_EXPERT = 256
EPS = 1e-6
LOG2_E = 1.4426950408889634

LANES = 128
SUBLANES = 8
N_PAIRS = EXPERTS_PER_GROUP * (EXPERTS_PER_GROUP - 1) // 2
N_BUCKETS = N_GROUPS * N_PAIRS
assert N_BUCKETS <= LANES
ROPE_HALF = MLA_ROPE // 2
HEAD_PAIRS = SB_HEADS // 2
assert SB_HEADS == MLA_HEADS and 2 * SB_HEAD_DIM == LANES and 2 * MLA_V == LANES

ROW_TILE = 512
ATT_TILE = 256
ATT_CHUNK = LANES
ROUTE_PARTS = 1
MOE_TILE = 256
EXT_WIDTH = D_MODEL + LANES
SB_UNDERFLOW = 151.0
VMEM_LIMIT = 56 * 1024 * 1024


def _rms(x):
    return x * lax.rsqrt(jnp.mean(x * x, axis=-1, keepdims=True) + EPS)


def _cparams(*sem):
    return pltpu.CompilerParams(dimension_semantics=sem, vmem_limit_bytes=VMEM_LIMIT)


def _proj_kernel(x_ref, pos_ref, invf_ref, an_ref, wsb_ref, wvsb_ref, wcq_ref, wckv_ref, wkr_ref, qn_ref, wuq_ref,
                 kvn_ref, wuk_ref, wuv_ref, qsb_ref, ksb_ref, vsb_ref, qm_ref, km_ref, vm_ref):
    tk = ATT_TILE
    nt_dims = (((1,), (1,)), ((), ()))
    hb = (_rms(x_ref[...]) * an_ref[...]).astype(BF16)
    cq = jnp.dot(hb, wcq_ref[...], preferred_element_type=F32)
    ckv = jnp.dot(hb, wckv_ref[...], preferred_element_type=F32)
    kr = jnp.dot(hb, wkr_ref[...], preferred_element_type=F32)
    sb = jnp.dot(hb, wsb_ref[...], preferred_element_type=F32)
    qsb_ref[...] = (sb[:, :SB_WIDTH] * (SB_HEAD_DIM ** -0.5 * LOG2_E)).astype(BF16)
    ksb_ref[...] = sb[:, SB_WIDTH:].astype(BF16)
    v_sb = lax.dot_general(wvsb_ref[...], hb, nt_dims, preferred_element_type=F32).astype(BF16)
    for kb in range(v_sb.shape[1] // tk):
        vsb_ref[kb] = v_sb[:, kb * tk:(kb + 1) * tk]

    q = jnp.dot((_rms(cq) * qn_ref[...]).astype(BF16), wuq_ref[...], preferred_element_type=F32)
    ckn = (_rms(ckv) * kvn_ref[...]).astype(BF16)
    kn = jnp.dot(ckn, wuk_ref[...], preferred_element_type=F32)
    v_m = lax.dot_general(wuv_ref[...], ckn, nt_dims, preferred_element_type=F32)
    ones_rows = (lax.broadcasted_iota(jnp.int32, (MLA_HEADS * LANES, 1), 0) % LANES) >= MLA_V
    v_m = jnp.where(ones_rows, 1.0, v_m).astype(BF16)
    for kb in range(v_m.shape[1] // tk):
        vm_ref[kb] = v_m[:, kb * tk:(kb + 1) * tk]

    ang = pos_ref[...].astype(F32) * invf_ref[...]
    cos, sin = jnp.cos(ang), jnp.sin(ang)
    lane = lax.broadcasted_iota(jnp.int32, (1, LANES), 1)
    x1_lanes = (lane >= MLA_NOPE) & (lane < MLA_NOPE + ROPE_HALF)
    x2_lanes = (lane >= MLA_NOPE + ROPE_HALF) & (lane < MLA_NOPE + MLA_ROPE)
    c_tab = jnp.where(lane < MLA_NOPE, 1.0, jnp.where(x1_lanes | x2_lanes, cos, 0.0))
    s_from_x2 = jnp.where(x1_lanes, -sin, 0.0)
    s_from_x1 = jnp.where(x2_lanes, sin, 0.0)

    def rope(t):
        return (t * c_tab + pltpu.roll(t, LANES - ROPE_HALF, 1) * s_from_x2
                + pltpu.roll(t, ROPE_HALF, 1) * s_from_x1)

    k_rope = rope(kr)
    q_scale = (MLA_NOPE + MLA_ROPE) ** -0.5 * LOG2_E
    for h in range(MLA_HEADS):
        blk = slice(h * LANES, (h + 1) * LANES)
        qm_ref[:, blk] = (rope(q[:, blk]) * q_scale).astype(BF16)
        km_ref[:, blk] = (kn[:, blk] + k_rope).astype(BF16)


def _projections(x2d, pos2d, invf, an, w):
    T = x2d.shape[0]
    tm = ROW_TILE
    tk = ATT_TILE
    row = lambda n: pl.BlockSpec((tm, n), lambda i: (i, 0))
    slab = lambda n: pl.BlockSpec((tm // tk, n, tk), lambda i: (i, 0, 0))
    full = lambda a: pl.BlockSpec(a.shape, lambda i: (0,) * a.ndim)
    ins = [x2d, pos2d, invf, an, w["sb"], w["vsb"], w["cq"], w["ckv"], w["kr"], w["qn"], w["uq"], w["kvn"],
           w["uk"], w["uv"]]
    in_specs = [row(D_MODEL), row(1)] + [full(a) for a in ins[2:]]
    wide = MLA_HEADS * LANES
    row_out = lambda n: jax.ShapeDtypeStruct((T, n), BF16)
    slab_out = lambda n: jax.ShapeDtypeStruct((T // tk, n, tk), BF16)
    return pl.pallas_call(
        _proj_kernel,
        grid=(T // tm,),
        in_specs=in_specs,
        out_specs=[row(SB_WIDTH), row(SB_WIDTH), slab(SB_WIDTH), row(wide), row(wide), slab(wide)],
        out_shape=[row_out(SB_WIDTH), row_out(SB_WIDTH), slab_out(SB_WIDTH), row_out(wide), row_out(wide),
                   slab_out(wide)],
        compiler_params=_cparams("parallel"),
        name="proj",
    )(*ins)


def _softplus2(z):
    sign_bit = jnp.uint32(0x80000000)
    neg_abs = lax.bitcast_convert_type(lax.bitcast_convert_type(z, jnp.uint32) | sign_bit, F32)
    return jnp.maximum(z, 0.0) + jnp.log2(1.0 + jnp.exp2(neg_abs))


def _sb_kernel(q_ref, k_ref, v_ref, o_ref, z_ref, w_ref, acc_ref, c_ref, scale_ref):
    tq = tk = ATT_TILE
    cw = ATT_CHUNK
    i = pl.program_id(1)
    lane = lax.broadcasted_iota(jnp.int32, (1, LANES), 1)
    in_head = (lane < SB_HEAD_DIM, lane >= SB_HEAD_DIM)
    later = (lax.broadcasted_iota(jnp.int32, (tk, tk), 1)
             > lax.broadcasted_iota(jnp.int32, (tk, tk), 0)).astype(BF16)
    key_idx = lax.broadcasted_iota(jnp.int32, (tk, cw), 0)
    qry_idx = lax.broadcasted_iota(jnp.int32, (tk, cw), 1)
    strict = [key_idx < qry_idx + c * cw for c in range(tq // cw)]
    nt_dims = (((1,), (1,)), ((), ()))
    pair_lanes = lambda h: slice((h // 2) * LANES, (h // 2 + 1) * LANES)

    def add_values(j_blk, h):
        v_head = v_ref[j_blk, h * SB_HEAD_DIM:(h + 1) * SB_HEAD_DIM, :]
        acc_ref[h] += jnp.dot(v_head, w_ref[h], preferred_element_type=F32) * scale_ref[h]

    def visit(j, j_prev, diagonal):
        keys = pl.ds(pl.multiple_of(j * tk, tk), tk)
        for h in range(SB_HEADS):
            q_pair = q_ref[:, pair_lanes(h)]
            qh = jnp.where(in_head[h % 2], q_pair, jnp.zeros_like(q_pair))
            z_ref[h] = lax.dot_general(k_ref[keys, pair_lanes(h)], qh, nt_dims, preferred_element_type=F32)
            if diagonal:
                acc_ref[h] = jnp.zeros((SB_HEAD_DIM, tq), F32)
            else:
                add_values(j_prev, h)
        for h in range(SB_HEADS):
            for c in range(tq // cw):
                cols = slice(c * cw, (c + 1) * cw)
                nk = min(tk, (c + 1) * cw) if diagonal else tk
                z = z_ref[h, :nk, cols]
                sp = _softplus2(z)
                if diagonal:
                    sp = jnp.where(strict[c][:nk], sp, 0.0)
                    if nk < tk:
                        w_ref[h, nk:, cols] = jnp.zeros((tk - nk, cw), BF16)
                z_ref[h, :nk, cols] = z - sp
                w_ref[h, :nk, cols] = sp.astype(BF16)
        c_low = None
        for h in range(SB_HEADS):
            suffix = jnp.dot(later, w_ref[h], preferred_element_type=F32)
            block_sum = suffix[0:1, :] + w_ref[h, 0:1, :].astype(F32)
            if diagonal:
                scale_ref[h] = jnp.ones((1, tq), F32)
                c_new = block_sum
            else:
                c_old = c_ref[h]
                scale_ref[h] = jnp.exp2(-c_old)
                c_new = c_old + block_sum
            c_ref[h] = c_new
            c_low = c_new if c_low is None else jnp.minimum(c_low, c_new)
            for c in range(tq // cw):
                cols = slice(c * cw, (c + 1) * cw)
                nk = min(tk, (c + 1) * cw) if diagonal else tk
                a = jnp.exp2(z_ref[h, :nk, cols] - suffix[:nk, cols])
                if diagonal:
                    a = jnp.where(strict[c][:nk], a, 0.0)
                w_ref[h, :nk, cols] = a.astype(BF16)
        return jnp.min(c_low)

    def cond(carry):
        j, _, c_min = carry
        return (j >= 0) & (c_min < SB_UNDERFLOW)

    def body(carry):
        j, j_prev, _ = carry
        return j - 1, j, visit(j, j_prev, diagonal=False)

    _, j_last, _ = lax.while_loop(cond, body, (i - 1, i, visit(i, i, diagonal=True)))
    for h in range(SB_HEADS):
        add_values(j_last, h)
    for p in range(HEAD_PAIRS):
        o_ref[:, p * LANES:(p + 1) * LANES] = jnp.concatenate([acc_ref[2 * p], acc_ref[2 * p + 1]], axis=0).T


def _sb_attention(q, k, v, B, S):
    tq = tk = ATT_TILE
    nq = S // tq
    return pl.pallas_call(
        _sb_kernel,
        grid=(B, nq),
        in_specs=[
            pl.BlockSpec((tq, SB_WIDTH), lambda b, i: (b * nq + i, 0)),
            pl.BlockSpec((S, SB_WIDTH), lambda b, i: (b, 0)),
            pl.BlockSpec((S // tk, SB_WIDTH, tk), lambda b, i: (b, 0, 0)),
        ],
        out_specs=pl.BlockSpec((tq, SB_WIDTH), lambda b, i: (b * nq + i, 0)),
        out_shape=jax.ShapeDtypeStruct((B * S, SB_WIDTH), F32),
        scratch_shapes=[
            pltpu.VMEM((SB_HEADS, tk, tq), F32),
            pltpu.VMEM((SB_HEADS, tk, tq), BF16),
            pltpu.VMEM((SB_HEADS, SB_HEAD_DIM, tq), F32),
            pltpu.VMEM((SB_HEADS, 1, tq), F32),
            pltpu.VMEM((SB_HEADS, 1, tq), F32),
        ],
        compiler_params=_cparams("parallel", "parallel"),
        name="sb_attn",
    )(q, k, v)


def _mla_kernel(q_ref, k_ref, v_ref, o_ref, s_ref, p_ref, acc_ref, m_ref):
    tq = tk = ATT_TILE
    cw = ATT_CHUNK
    i = pl.program_id(1)
    key_idx = lax.broadcasted_iota(jnp.int32, (tk, cw), 0)
    qry_idx = lax.broadcasted_iota(jnp.int32, (tk, cw), 1)
    causal = [key_idx <= qry_idx + c * cw for c in range(tq // cw)]
    nt_dims = (((1,), (1,)), ((), ()))
    head_lanes = lambda h: slice(h * LANES, (h + 1) * LANES)

    def visit(j, j_prev, diagonal):
        keys = pl.ds(pl.multiple_of(j * tk, tk), tk)

        for h in range(MLA_HEADS):
            s_ref[h] = lax.dot_general(k_ref[keys, head_lanes(h)], q_ref[:, head_lanes(h)], nt_dims,
                                       preferred_element_type=F32)
        for h in range(MLA_HEADS):
            if diagonal:
                acc_ref[h] = jnp.zeros((LANES, tq), F32)
            else:
                acc_ref[h] += jnp.dot(v_ref[j_prev, head_lanes(h), :], p_ref[h], preferred_element_type=F32)
        for h in range(MLA_HEADS):
            for c in range(tq // cw):
                cols = slice(c * cw, (c + 1) * cw)
                nk = min(tk, (c + 1) * cw) if diagonal else tk
                s = s_ref[h, :nk, cols]
                if diagonal:
                    s = jnp.where(causal[c][:nk], s, -jnp.inf)
                    m_new = jnp.max(s, axis=0, keepdims=True)
                    if nk < tk:
                        p_ref[h, nk:, cols] = jnp.zeros((tk - nk, cw), BF16)
                else:
                    m_old = m_ref[h, :, cols]
                    m_new = jnp.maximum(m_old, jnp.max(s, axis=0, keepdims=True))
                    acc_ref[h, :, cols] *= jnp.exp2(m_old - m_new)
                p_ref[h, :nk, cols] = jnp.exp2(s - m_new).astype(BF16)
                m_ref[h, :, cols] = m_new

    visit(i, i, diagonal=True)

    def body(j, carry):
        visit(j, jnp.where(j == 0, i, j - 1), diagonal=False)
        return carry

    lax.fori_loop(0, i, body, 0)
    j_last = jnp.where(i == 0, i, i - 1)
    for h in range(MLA_HEADS):
        acc_ref[h] += jnp.dot(v_ref[j_last, head_lanes(h), :], p_ref[h], preferred_element_type=F32)
    for p in range(HEAD_PAIRS):
        outs = [acc_ref[2 * p + e, :MLA_V] / acc_ref[2 * p + e, MLA_V:] for e in range(2)]
        o_ref[:, p * LANES:(p + 1) * LANES] = jnp.concatenate(outs, axis=0).T


def _mla_attention(q, k, v, B, S):
    tq = tk = ATT_TILE
    nq = S // tq
    width = MLA_HEADS * LANES
    return pl.pallas_call(
        _mla_kernel,
        grid=(B, nq),
        in_specs=[
            pl.BlockSpec((tq, width), lambda b, i: (b * nq + i, 0)),
            pl.BlockSpec((S, width), lambda b, i: (b, 0)),
            pl.BlockSpec((S // tk, width, tk), lambda b, i: (b, 0, 0)),
        ],
        out_specs=pl.BlockSpec((tq, MLA_WIDTH), lambda b, i: (b * nq + i, 0)),
        out_shape=jax.ShapeDtypeStruct((B * S, MLA_WIDTH), F32),
        scratch_shapes=[
            pltpu.VMEM((MLA_HEADS, tk, tq), F32),
            pltpu.VMEM((MLA_HEADS, tk, tq), BF16),
            pltpu.VMEM((MLA_HEADS, LANES, tq), F32),
            pltpu.VMEM((MLA_HEADS, 1, tq), F32),
        ],
        compiler_params=_cparams("parallel", "parallel"),
        name="mla_attn",
    )(q, k, v)


ROUTER_ROWS = SUBLANES * (1 + N_GROUPS)


def _split_bf16(a):
    hi = a.astype(BF16)
    return hi, (a - hi.astype(F32)).astype(BF16)


def _route_kernel(osb_ref, omla_ref, x_ref, g_sb_ref, g_mla_ref, wout_ref, fn_ref, wr_ref, br_ref,
                  x1_ref, hext_ref, bucket_ref, rank_ref, counts_ref, carry_ref):
    tm = ROW_TILE

    @pl.when(pl.program_id(0) == 0)
    def _():
        carry_ref[...] = jnp.zeros_like(carry_ref)

    part = tm // ROUTE_PARTS
    parts = [slice(p * part, (p + 1) * part) for p in range(ROUTE_PARTS)]
    x1s = []
    for rows in parts:
        o = jnp.concatenate([_rms(osb_ref[rows]) * g_sb_ref[...], _rms(omla_ref[rows]) * g_mla_ref[...]], axis=-1)
        x1 = x_ref[rows] + jnp.dot(o.astype(BF16), wout_ref[...], preferred_element_type=F32)
        x1_ref[rows] = x1
        x1s.append(x1)
    logit_parts = []
    for rows, x1 in zip(parts, x1s):
        h = _rms(x1) * fn_ref[...]
        hext_ref[rows, :D_MODEL] = h
        h_hi, h_lo = _split_bf16(h)
        both = jnp.dot(h_hi, wr_ref[...], preferred_element_type=F32)
        logit_parts.append((both[:, :LANES] + both[:, LANES:]
                            + jnp.dot(h_lo, wr_ref[:, :LANES], preferred_element_type=F32)) + br_ref[...])
    for p, (rows, logits) in enumerate(zip(parts, logit_parts)):
        _route_part(logits, rows, slice(p * part, (p + 1) * part), hext_ref, bucket_ref, rank_ref, carry_ref)
    counts_ref[...] = carry_ref[...]


def _route_part(logits, rows, cols, hext_ref, bucket_ref, rank_ref, carry_ref):
    tm = logits.shape[0]
    lt = logits.T

    rid = lax.broadcasted_iota(jnp.int32, (SUBLANES, tm), 0)
    g_logit = jnp.where(rid < N_GROUPS, lt[:SUBLANES], -jnp.inf)
    g_exp = jnp.exp(g_logit - jnp.max(g_logit, axis=0, keepdims=True))
    p_group = g_exp / jnp.sum(g_exp, axis=0, keepdims=True)
    g_val = jnp.max(p_group, axis=0, keepdims=True)
    g_idx = jnp.min(jnp.where(p_group == g_val, rid, SUBLANES), axis=0, keepdims=True)
    local = lt[SUBLANES * N_GROUPS:SUBLANES * (N_GROUPS + 1)]
    for g in range(N_GROUPS - 2, -1, -1):
        local = jnp.where(g_idx == g, lt[SUBLANES * (g + 1):SUBLANES * (g + 2)], local)
    e_exp = jnp.exp(local - jnp.max(local, axis=0, keepdims=True))
    p_exp = e_exp / jnp.sum(e_exp, axis=0, keepdims=True)
    v1 = jnp.max(p_exp, axis=0, keepdims=True)
    i1 = jnp.min(jnp.where(p_exp == v1, rid, SUBLANES), axis=0, keepdims=True)
    rest = jnp.where(rid == i1, -1.0, p_exp)
    v2 = jnp.max(rest, axis=0, keepdims=True)
    i2 = jnp.min(jnp.where(rest == v2, rid, SUBLANES), axis=0, keepdims=True)
    den = v1 + v2
    w1 = g_val * v1 / den
    w2 = g_val * v2 / den
    first_lower = i1 < i2
    e_lo = jnp.where(first_lower, i1, i2)
    e_hi = jnp.where(first_lower, i2, i1)
    w_lo = jnp.where(first_lower, w1, w2)
    w_hi = jnp.where(first_lower, w2, w1)
    pair = ((e_lo * (2 * EXPERTS_PER_GROUP - 1 - e_lo)) >> 1) + (e_hi - e_lo - 1)
    bucket = g_idx * N_PAIRS + pair
    bucket_ref[0, :, cols] = bucket

    rid_full = lax.broadcasted_iota(jnp.int32, (LANES, tm), 0)
    w_rows = jnp.where(rid_full == 0, w_lo, jnp.where(rid_full == 1, w_hi, 0.0))
    hext_ref[rows, D_MODEL:] = w_rows.T

    onehot = (rid_full == bucket).astype(F32)
    trow = lax.broadcasted_iota(jnp.int32, (tm, tm), 0)
    tcol = lax.broadcasted_iota(jnp.int32, (tm, tm), 1)
    earlier = (trow < tcol).astype(BF16)
    before = jnp.dot(onehot.astype(BF16), earlier, preferred_element_type=F32) + carry_ref[...]
    rank_ref[0, :, cols] = jnp.sum(onehot * before, axis=0, keepdims=True).astype(jnp.int32)
    carry_ref[...] += jnp.sum(onehot, axis=1, keepdims=True)


def _route(o_sb, o_mla, x2d, g_sb, g_mla, w_out, fn, wr, br):
    T = x2d.shape[0]
    tm = ROW_TILE
    nt = T // tm
    row = lambda n: pl.BlockSpec((tm, n), lambda i: (i, 0))
    full = lambda a: pl.BlockSpec(a.shape, lambda i: (0,) * a.ndim)
    tok = pl.BlockSpec((1, 1, tm), lambda i: (i, 0, 0))
    ins = [o_sb, o_mla, x2d, g_sb, g_mla, w_out, fn, wr, br]
    return pl.pallas_call(
        _route_kernel,
        grid=(nt,),
        in_specs=[row(SB_WIDTH), row(MLA_WIDTH), row(D_MODEL)] + [full(a) for a in ins[3:]],
        out_specs=[row(D_MODEL), row(EXT_WIDTH), tok, tok, pl.BlockSpec((LANES, 1), lambda i: (0, 0))],
        out_shape=[
            jax.ShapeDtypeStruct((T, D_MODEL), F32),
            jax.ShapeDtypeStruct((T, EXT_WIDTH), F32),
            jax.ShapeDtypeStruct((nt, 1, tm), jnp.int32),
            jax.ShapeDtypeStruct((nt, 1, tm), jnp.int32),
            jax.ShapeDtypeStruct((LANES, 1), F32),
        ],
        scratch_shapes=[pltpu.VMEM((LANES, 1), F32)],
        compiler_params=_cparams("arbitrary"),
        name="route",
    )(*ins)


DEST_TILES = 8


def _dest_kernel(bucket_ref, rank_ref, offs_ref, dest_ref):
    tm = bucket_ref.shape[-1]
    rid = lax.broadcasted_iota(jnp.int32, (LANES, tm), 0)
    for t in range(bucket_ref.shape[0]):
        start = jnp.sum(jnp.where(rid == bucket_ref[t], offs_ref[...], 0), axis=0, keepdims=True)
        dest_ref[t] = start + rank_ref[t]


def _dest_rows(bucket, rank, offsets):
    nt, _, tm = bucket.shape
    assert nt % DEST_TILES == 0
    tok = pl.BlockSpec((DEST_TILES, 1, tm), lambda i: (i, 0, 0))
    return pl.pallas_call(
        _dest_kernel,
        grid=(nt // DEST_TILES,),
        in_specs=[tok, tok, pl.BlockSpec((LANES, 1), lambda i: (0, 0))],
        out_specs=tok,
        out_shape=jax.ShapeDtypeStruct((nt, 1, tm), jnp.int32),
        compiler_params=_cparams("parallel"),
        name="dest",
    )(bucket, rank, offsets)


def _scatter_kernel(dest_ref, h_ref, init_ref, xs_ref, sem):
    del init_ref
    tm = h_ref.shape[0]
    base = pl.program_id(0) * tm

    for r in range(tm):
        pltpu.make_async_copy(h_ref.at[pl.ds(r, 1)], xs_ref.at[pl.ds(dest_ref[base + r], 1)], sem).start()
    pltpu.make_async_copy(h_ref, xs_ref.at[pl.ds(0, tm)], sem).wait()


def _scatter_rows(dest, hext, n_rows):
    T, W = hext.shape
    tm = ROW_TILE
    init = jnp.zeros((n_rows, W), hext.dtype)
    return pl.pallas_call(
        _scatter_kernel,
        grid_spec=pltpu.PrefetchScalarGridSpec(
            num_scalar_prefetch=1,
            grid=(T // tm,),
            in_specs=[pl.BlockSpec((tm, W), lambda i, d: (i, 0)), pl.BlockSpec(memory_space=pl.ANY)],
            out_specs=pl.BlockSpec(memory_space=pl.ANY),
            scratch_shapes=[pltpu.SemaphoreType.DMA],
        ),
        out_shape=jax.ShapeDtypeStruct((n_rows, W), hext.dtype),
        input_output_aliases={2: 0},
        compiler_params=_cparams("arbitrary"),
        name="scatter",
    )(dest, hext, init)


def _moe_kernel(elo_ref, ehi_ref, nt_ref, xs_ref, wgu_lo_ref, wd_lo_ref, wgu_hi_ref, wd_hi_ref, ys_ref):
    del elo_ref, ehi_ref
    used = pl.program_id(0) < nt_ref[0]

    @pl.when(jnp.logical_not(used))
    def _():
        ys_ref[...] = jnp.zeros_like(ys_ref)

    @pl.when(used)
    def _():
        h = xs_ref[:, :D_MODEL].astype(BF16)
        gates = xs_ref[:, D_MODEL:]
        gu_lo = jnp.dot(h, wgu_lo_ref[0], preferred_element_type=F32)
        gu_hi = jnp.dot(h, wgu_hi_ref[0], preferred_element_type=F32)

        def down(gu, wd_ref):
            g, u = gu[:, :D_EXPERT], gu[:, D_EXPERT:]
            hid = (g * jax.nn.sigmoid(g)) * u
            return jnp.dot(hid.astype(BF16), wd_ref[0], preferred_element_type=F32)

        ys_ref[...] = down(gu_lo, wd_lo_ref) * gates[:, 0:1] + down(gu_hi, wd_hi_ref) * gates[:, 1:2]


def _moe(tile_elo, tile_ehi, n_tiles_used, xs, w_gu, w_d):
    n_rows = xs.shape[0]
    tile = MOE_TILE
    rows = lambda w: pl.BlockSpec((tile, w), lambda i, elo, ehi, nt: (jnp.minimum(i, nt[0] - 1), 0))
    wspec = lambda shape, which: pl.BlockSpec(
        (1,) + shape, lambda i, elo, ehi, nt: ((elo, ehi)[which][i], 0, 0))
    return pl.pallas_call(
        _moe_kernel,
        grid_spec=pltpu.PrefetchScalarGridSpec(
            num_scalar_prefetch=3,
            grid=(n_rows // tile,),
            in_specs=[
                rows(EXT_WIDTH),
                wspec((D_MODEL, 2 * D_EXPERT), 0), wspec((D_EXPERT, D_MODEL), 0),
                wspec((D_MODEL, 2 * D_EXPERT), 1), wspec((D_EXPERT, D_MODEL), 1),
            ],
            out_specs=pl.BlockSpec((tile, D_MODEL), lambda i, elo, ehi, nt: (i, 0)),
        ),
        out_shape=jax.ShapeDtypeStruct((n_rows, D_MODEL), F32),
        compiler_params=_cparams("arbitrary"),
        name="moe",
    )(tile_elo, tile_ehi, n_tiles_used, xs, w_gu, w_d, w_gu, w_d)


def _final_kernel(dest_ref, ys_ref, x1_ref, fn_ref, o_ref, buf_ref, sem):
    tm = x1_ref.shape[0]
    i = pl.program_id(0)
    slot = i % 2

    def start_gather(t, s):
        for r in range(tm):
            pltpu.make_async_copy(ys_ref.at[pl.ds(dest_ref[t * tm + r], 1)], buf_ref.at[s, pl.ds(r, 1)],
                                  sem.at[s]).start()

    @pl.when(i == 0)
    def _():
        start_gather(0, 0)

    @pl.when(i + 1 < pl.num_programs(0))
    def _():
        start_gather(i + 1, 1 - slot)

    pltpu.make_async_copy(ys_ref.at[pl.ds(0, tm)], buf_ref.at[slot], sem.at[slot]).wait()
    o_ref[...] = _rms(x1_ref[...] + buf_ref[slot]) * fn_ref[...]


def _final(dest, ys, x1, fn):
    T, D = x1.shape
    tm = ROW_TILE
    return pl.pallas_call(
        _final_kernel,
        grid_spec=pltpu.PrefetchScalarGridSpec(
            num_scalar_prefetch=1,
            grid=(T // tm,),
            in_specs=[
                pl.BlockSpec(memory_space=pl.ANY),
                pl.BlockSpec((tm, D), lambda i, d: (i, 0)),
                pl.BlockSpec((1, D), lambda i, d: (0, 0)),
            ],
            out_specs=pl.BlockSpec((tm, D), lambda i, d: (i, 0)),
            scratch_shapes=[pltpu.VMEM((2, tm, D), F32), pltpu.SemaphoreType.DMA((2,))],
        ),
        out_shape=jax.ShapeDtypeStruct((T, D), F32),
        compiler_params=_cparams("arbitrary"),
        name="final",
    )(dest, ys, x1, fn)


def _pair_tables():
    lo, hi = [], []
    for g in range(N_GROUPS):
        for a in range(EXPERTS_PER_GROUP):
            for b in range(a + 1, EXPERTS_PER_GROUP):
                lo.append(g * EXPERTS_PER_GROUP + a)
                hi.append(g * EXPERTS_PER_GROUP + b)
    return np.asarray(lo, np.int32), np.asarray(hi, np.int32)


def _attention_weights(w_in, q_norm, w_uq, kv_norm, w_ukv):
    D = w_in.shape[0]
    c0 = 3 * SB_WIDTH
    zeros = lambda r, c: jnp.zeros((r, c), F32)
    w_kr = jnp.concatenate(
        [zeros(D, MLA_NOPE), w_in[:, c0 + Q_LORA + KV_LORA:], zeros(D, LANES - MLA_NOPE - MLA_ROPE)], axis=1)
    dq = MLA_NOPE + MLA_ROPE
    uq = jnp.concatenate(
        [jnp.concatenate([w_uq[:, h * dq:(h + 1) * dq], zeros(Q_LORA, LANES - dq)], axis=1)
         for h in range(MLA_HEADS)], axis=1)
    dkv = MLA_NOPE + MLA_V
    uk = jnp.concatenate(
        [jnp.concatenate([w_ukv[:, h * dkv:h * dkv + MLA_NOPE], zeros(KV_LORA, LANES - MLA_NOPE)], axis=1)
         for h in range(MLA_HEADS)], axis=1)
    uv = jnp.concatenate(
        [jnp.concatenate([w_ukv[:, h * dkv + MLA_NOPE:(h + 1) * dkv].T, zeros(LANES - MLA_V, KV_LORA)], axis=0)
         for h in range(MLA_HEADS)], axis=0)
    return {
        "sb": w_in[:, :2 * SB_WIDTH].astype(BF16),
        "vsb": w_in[:, 2 * SB_WIDTH:c0].T.astype(BF16),
        "cq": w_in[:, c0:c0 + Q_LORA].astype(BF16),
        "ckv": w_in[:, c0 + Q_LORA:c0 + Q_LORA + KV_LORA].astype(BF16),
        "kr": w_kr.astype(BF16),
        "qn": q_norm[None, :],
        "uq": uq.astype(BF16),
        "kvn": kv_norm[None, :],
        "uk": uk.astype(BF16),
        "uv": uv.astype(BF16),
    }


def _router_weights(w_group, b_group, w_expert, b_expert):
    D = w_group.shape[0]
    pad_g = SUBLANES - N_GROUPS
    pad_e = LANES - SUBLANES - N_EXPERTS
    w = jnp.concatenate([w_group, jnp.zeros((D, pad_g), F32), w_expert, jnp.zeros((D, pad_e), F32)], axis=1)
    b = jnp.concatenate([b_group, jnp.zeros((pad_g,), F32), b_expert, jnp.zeros((pad_e,), F32)])[None, :]
    return jnp.concatenate(_split_bf16(w), axis=1), b


def _bucket_layout(counts, n_tiles):
    c = counts[:N_BUCKETS, 0].astype(jnp.int32)
    tiles = (c + MOE_TILE - 1) // MOE_TILE
    tile_end = jnp.cumsum(tiles)
    starts = (tile_end - tiles) * MOE_TILE
    offsets = jnp.concatenate([starts, jnp.zeros((LANES - N_BUCKETS,), jnp.int32)])[:, None]
    n_used = tile_end[-1]
    tile_id = jnp.minimum(jnp.arange(n_tiles, dtype=jnp.int32), n_used - 1)
    tile_bucket = jnp.sum(tile_id[:, None] >= tile_end[None, :], axis=1)
    onehot = tile_bucket[:, None] == jnp.arange(N_BUCKETS)[None, :]
    pick = lambda table: jnp.sum(jnp.where(onehot, table[None, :], 0), axis=1).astype(jnp.int32)
    pair_lo, pair_hi = _pair_tables()
    return offsets, pick(jnp.asarray(pair_lo)), pick(jnp.asarray(pair_hi)), n_used[None]


def kernel(x, positions, attn_norm, w_in, q_norm, w_uq, kv_norm, w_ukv, sb_out_norm, mla_out_norm, w_out,
           ffn_norm, w_group_router, b_group_router, w_expert_router, b_expert_router, w_gate, w_up, w_down,
           final_norm):
    B, S, D = x.shape
    T = B * S
    depth = w_in.shape[0]
    assert D == D_MODEL and T % ROW_TILE == 0 and S % ATT_TILE == 0
    assert depth == 1, "the final norm is fused into the last layer's gather kernel"
    n_sorted_tiles = T // MOE_TILE + N_BUCKETS
    n_sorted_rows = n_sorted_tiles * MOE_TILE

    lane = jnp.arange(LANES)
    invf = (ROPE_BASE ** (-(lane % ROPE_HALF).astype(F32) / ROPE_HALF))[None, :]
    pos2d = positions.reshape(T, 1)
    x2d = x.reshape(T, D)
    for l in range(depth):
        aw = _attention_weights(w_in[l], q_norm[l], w_uq[l], kv_norm[l], w_ukv[l])
        q_sb, k_sb, v_sb, q_m, k_m, v_m = _projections(x2d, pos2d, invf, attn_norm[l][None, :], aw)
        o_sb = _sb_attention(q_sb, k_sb, v_sb, B, S)
        o_mla = _mla_attention(q_m, k_m, v_m, B, S)

        wr, br = _router_weights(w_group_router[l], b_group_router[l], w_expert_router[l], b_expert_router[l])
        x1, hext, bucket, rank, counts = _route(
            o_sb, o_mla, x2d, sb_out_norm[l][None, :], mla_out_norm[l][None, :], w_out[l].astype(BF16),
            ffn_norm[l][None, :], wr, br)
        offsets, tile_elo, tile_ehi, n_used = _bucket_layout(counts, n_sorted_tiles)
        dest = _dest_rows(bucket, rank, offsets).reshape(T)

        xs = _scatter_rows(dest, hext, n_sorted_rows)
        w_gu = jnp.concatenate([w_gate[l], w_up[l]], axis=-1).astype(BF16)
        ys = _moe(tile_elo, tile_ehi, n_used, xs, w_gu, w_down[l].astype(BF16))
        x2d = _final(dest, ys, x1, final_norm[None, :])
    return x2d.reshape(B, S, D)
```

```python
import functools

import jax
import jax.numpy as jnp
import numpy as np
from jax import lax
from jax.experimental import pallas as pl
from jax.experimental.pallas import tpu as pltpu

F32 = jnp.float32
BF16 = jnp.bfloat16

D_MODEL = 1024
SB_HEADS = 8
SB_HEAD_DIM = 64
SB_WIDTH = SB_HEADS * SB_HEAD_DIM
MLA_HEADS = 8
MLA_NOPE = 64
MLA_ROPE = 32
MLA_V = 64
MLA_WIDTH = MLA_HEADS * MLA_V
Q_LORA = 256
KV_LORA = 128
ROPE_BASE = 10000.0
N_GROUPS = 4
EXPERTS_PER_GROUP = 8
N_EXPERTS = N_GROUPS * EXPERTS_PER_GROUP
D_EXPERT = 256
EPS = 1e-6
LOG2_E = 1.4426950408889634

LANES = 128
SUBLANES = 8
N_PAIRS = EXPERTS_PER_GROUP * (EXPERTS_PER_GROUP - 1) // 2
N_BUCKETS = N_GROUPS * N_PAIRS
assert N_BUCKETS <= LANES
ROPE_HALF = MLA_ROPE // 2
HEAD_PAIRS = SB_HEADS // 2
assert SB_HEADS == MLA_HEADS and 2 * SB_HEAD_DIM == LANES and 2 * MLA_V == LANES

ROW_TILE = 512
DISPATCH_TILE = 1024
ATT_TILE = 256
ATT_CHUNK = LANES
ROUTE_PARTS = 1
MOE_TILE = 256
EXT_WIDTH = D_MODEL + LANES
SB_UNDERFLOW = 151.0
VMEM_LIMIT = 56 * 1024 * 1024


def _rms(x):
    return x * lax.rsqrt(jnp.mean(x * x, axis=-1, keepdims=True) + EPS)


def _cparams(*sem):
    return pltpu.CompilerParams(dimension_semantics=sem, vmem_limit_bytes=VMEM_LIMIT)


def _proj_kernel(x_ref, pos_ref, invf_ref, an_ref, wsb_ref, wvsb_ref, wcq_ref, wckv_ref, wkr_ref, qn_ref, wuq_ref,
                 kvn_ref, wuk_ref, wuv_ref, qsb_ref, ksb_ref, vsb_ref, qm_ref, km_ref, vm_ref):
    tk = ATT_TILE
    nt_dims = (((1,), (1,)), ((), ()))
    hb = (_rms(x_ref[...]) * an_ref[...]).astype(BF16)
    cq = jnp.dot(hb, wcq_ref[...], preferred_element_type=F32)
    ckv = jnp.dot(hb, wckv_ref[...], preferred_element_type=F32)
    kr = jnp.dot(hb, wkr_ref[...], preferred_element_type=F32)
    sb = jnp.dot(hb, wsb_ref[...], preferred_element_type=F32)
    qsb_ref[...] = (sb[:, :SB_WIDTH] * (SB_HEAD_DIM ** -0.5 * LOG2_E)).astype(BF16)
    ksb_ref[...] = sb[:, SB_WIDTH:].astype(BF16)
    v_sb = lax.dot_general(wvsb_ref[...], hb, nt_dims, preferred_element_type=F32).astype(BF16)
    for kb in range(v_sb.shape[1] // tk):
        vsb_ref[kb] = v_sb[:, kb * tk:(kb + 1) * tk]

    q = jnp.dot((_rms(cq) * qn_ref[...]).astype(BF16), wuq_ref[...], preferred_element_type=F32)
    ckn = (_rms(ckv) * kvn_ref[...]).astype(BF16)
    kn = jnp.dot(ckn, wuk_ref[...], preferred_element_type=F32)
    v_m = lax.dot_general(wuv_ref[...], ckn, nt_dims, preferred_element_type=F32)
    ones_rows = (lax.broadcasted_iota(jnp.int32, (MLA_HEADS * LANES, 1), 0) % LANES) >= MLA_V
    v_m = jnp.where(ones_rows, 1.0, v_m).astype(BF16)
    for kb in range(v_m.shape[1] // tk):
        vm_ref[kb] = v_m[:, kb * tk:(kb + 1) * tk]

    ang = pos_ref[...].astype(F32) * invf_ref[...]
    cos, sin = jnp.cos(ang), jnp.sin(ang)
    lane = lax.broadcasted_iota(jnp.int32, (1, LANES), 1)
    x1_lanes = (lane >= MLA_NOPE) & (lane < MLA_NOPE + ROPE_HALF)
    x2_lanes = (lane >= MLA_NOPE + ROPE_HALF) & (lane < MLA_NOPE + MLA_ROPE)
    c_tab = jnp.where(lane < MLA_NOPE, 1.0, jnp.where(x1_lanes | x2_lanes, cos, 0.0))
    s_from_x2 = jnp.where(x1_lanes, -sin, 0.0)
    s_from_x1 = jnp.where(x2_lanes, sin, 0.0)

    def rope(t):
        return (t * c_tab + pltpu.roll(t, LANES - ROPE_HALF, 1) * s_from_x2
                + pltpu.roll(t, ROPE_HALF, 1) * s_from_x1)

    k_rope = rope(kr)
    q_scale = (MLA_NOPE + MLA_ROPE) ** -0.5 * LOG2_E
    for h in range(MLA_HEADS):
        blk = slice(h * LANES, (h + 1) * LANES)
        qm_ref[:, blk] = (rope(q[:, blk]) * q_scale).astype(BF16)
        km_ref[:, blk] = (kn[:, blk] + k_rope).astype(BF16)


def _projections(x2d, pos2d, invf, an, w):
    T = x2d.shape[0]
    tm = ROW_TILE
    tk = ATT_TILE
    row = lambda n: pl.BlockSpec((tm, n), lambda i: (i, 0))
    slab = lambda n: pl.BlockSpec((tm // tk, n, tk), lambda i: (i, 0, 0))
    full = lambda a: pl.BlockSpec(a.shape, lambda i: (0,) * a.ndim)
    ins = [x2d, pos2d, invf, an, w["sb"], w["vsb"], w["cq"], w["ckv"], w["kr"], w["qn"], w["uq"], w["kvn"],
           w["uk"], w["uv"]]
    in_specs = [row(D_MODEL), row(1)] + [full(a) for a in ins[2:]]
    wide = MLA_HEADS * LANES
    row_out = lambda n: jax.ShapeDtypeStruct((T, n), BF16)
    slab_out = lambda n: jax.ShapeDtypeStruct((T // tk, n, tk), BF16)
    return pl.pallas_call(
        _proj_kernel,
        grid=(T // tm,),
        in_specs=in_specs,
        out_specs=[row(SB_WIDTH), row(SB_WIDTH), slab(SB_WIDTH), row(wide), row(wide), slab(wide)],
        out_shape=[row_out(SB_WIDTH), row_out(SB_WIDTH), slab_out(SB_WIDTH), row_out(wide), row_out(wide),
                   slab_out(wide)],
        compiler_params=_cparams("parallel"),
        name="proj",
    )(*ins)


def _softplus2(z):
    sign_bit = jnp.uint32(0x80000000)
    neg_abs = lax.bitcast_convert_type(lax.bitcast_convert_type(z, jnp.uint32) | sign_bit, F32)
    return jnp.maximum(z, 0.0) + jnp.log2(1.0 + jnp.exp2(neg_abs))


def _sb_kernel(q_ref, k_ref, v_ref, o_ref, z_ref, w_ref, acc_ref, c_ref, scale_ref):
    tq = tk = ATT_TILE
    cw = ATT_CHUNK
    i = pl.program_id(1)
    lane = lax.broadcasted_iota(jnp.int32, (1, LANES), 1)
    in_head = (lane < SB_HEAD_DIM, lane >= SB_HEAD_DIM)
    later = (lax.broadcasted_iota(jnp.int32, (tk, tk), 1)
             > lax.broadcasted_iota(jnp.int32, (tk, tk), 0)).astype(BF16)
    key_idx = lax.broadcasted_iota(jnp.int32, (tk, cw), 0)
    qry_idx = lax.broadcasted_iota(jnp.int32, (tk, cw), 1)
    strict = [key_idx < qry_idx + c * cw for c in range(tq // cw)]
    nt_dims = (((1,), (1,)), ((), ()))
    pair_lanes = lambda h: slice((h // 2) * LANES, (h // 2 + 1) * LANES)

    def add_values(j_blk, h):
        v_head = v_ref[j_blk, h * SB_HEAD_DIM:(h + 1) * SB_HEAD_DIM, :]
        acc_ref[h] += jnp.dot(v_head, w_ref[h], preferred_element_type=F32) * scale_ref[h]

    def visit(j, j_prev, diagonal):
        keys = pl.ds(pl.multiple_of(j * tk, tk), tk)
        for h in range(SB_HEADS):
            q_pair = q_ref[:, pair_lanes(h)]
            qh = jnp.where(in_head[h % 2], q_pair, jnp.zeros_like(q_pair))
            z_ref[h] = lax.dot_general(k_ref[keys, pair_lanes(h)], qh, nt_dims, preferred_element_type=F32)
            if diagonal:
                acc_ref[h] = jnp.zeros((SB_HEAD_DIM, tq), F32)
            else:
                add_values(j_prev, h)
        for h in range(SB_HEADS):
            for c in range(tq // cw):
                cols = slice(c * cw, (c + 1) * cw)
                nk = min(tk, (c + 1) * cw) if diagonal else tk
                z = z_ref[h, :nk, cols]
                sp = _softplus2(z)
                if diagonal:
                    sp = jnp.where(strict[c][:nk], sp, 0.0)
                    if nk < tk:
                        w_ref[h, nk:, cols] = jnp.zeros((tk - nk, cw), BF16)
                z_ref[h, :nk, cols] = z - sp
                w_ref[h, :nk, cols] = sp.astype(BF16)
        c_low = None
        for h in range(SB_HEADS):
            suffix = jnp.dot(later, w_ref[h], preferred_element_type=F32)
            block_sum = suffix[0:1, :] + w_ref[h, 0:1, :].astype(F32)
            if diagonal:
                scale_ref[h] = jnp.ones((1, tq), F32)
                c_new = block_sum
            else:
                c_old = c_ref[h]
                scale_ref[h] = jnp.exp2(-c_old)
                c_new = c_old + block_sum
            c_ref[h] = c_new
            c_low = c_new if c_low is None else jnp.minimum(c_low, c_new)
            for c in range(tq // cw):
                cols = slice(c * cw, (c + 1) * cw)
                nk = min(tk, (c + 1) * cw) if diagonal else tk
                a = jnp.exp2(z_ref[h, :nk, cols] - suffix[:nk, cols])
                if diagonal:
                    a = jnp.where(strict[c][:nk], a, 0.0)
                w_ref[h, :nk, cols] = a.astype(BF16)
        return jnp.min(c_low)

    def cond(carry):
        j, _, c_min = carry
        return (j >= 0) & (c_min < SB_UNDERFLOW)

    def body(carry):
        j, j_prev, _ = carry
        return j - 1, j, visit(j, j_prev, diagonal=False)

    _, j_last, _ = lax.while_loop(cond, body, (i - 1, i, visit(i, i, diagonal=True)))
    for h in range(SB_HEADS):
        add_values(j_last, h)
    for p in range(HEAD_PAIRS):
        o_ref[:, p * LANES:(p + 1) * LANES] = jnp.concatenate([acc_ref[2 * p], acc_ref[2 * p + 1]], axis=0).T


def _sb_attention(q, k, v, B, S):
    tq = tk = ATT_TILE
    nq = S // tq
    return pl.pallas_call(
        _sb_kernel,
        grid=(B, nq),
        in_specs=[
            pl.BlockSpec((tq, SB_WIDTH), lambda b, i: (b * nq + i, 0)),
            pl.BlockSpec((S, SB_WIDTH), lambda b, i: (b, 0)),
            pl.BlockSpec((S // tk, SB_WIDTH, tk), lambda b, i: (b, 0, 0)),
        ],
        out_specs=pl.BlockSpec((tq, SB_WIDTH), lambda b, i: (b * nq + i, 0)),
        out_shape=jax.ShapeDtypeStruct((B * S, SB_WIDTH), F32),
        scratch_shapes=[
            pltpu.VMEM((SB_HEADS, tk, tq), F32),
            pltpu.VMEM((SB_HEADS, tk, tq), BF16),
            pltpu.VMEM((SB_HEADS, SB_HEAD_DIM, tq), F32),
            pltpu.VMEM((SB_HEADS, 1, tq), F32),
            pltpu.VMEM((SB_HEADS, 1, tq), F32),
        ],
        compiler_params=_cparams("parallel", "parallel"),
        name="sb_attn",
    )(q, k, v)


def _mla_kernel(q_ref, k_ref, v_ref, o_ref, s_ref, p_ref, acc_ref, m_ref):
    tq = tk = ATT_TILE
    cw = ATT_CHUNK
    i = pl.program_id(1)
    key_idx = lax.broadcasted_iota(jnp.int32, (tk, cw), 0)
    qry_idx = lax.broadcasted_iota(jnp.int32, (tk, cw), 1)
    causal = [key_idx <= qry_idx + c * cw for c in range(tq // cw)]
    nt_dims = (((1,), (1,)), ((), ()))
    head_lanes = lambda h: slice(h * LANES, (h + 1) * LANES)

    def visit(j, j_prev, diagonal):
        keys = pl.ds(pl.multiple_of(j * tk, tk), tk)

        for h in range(MLA_HEADS):
            s_ref[h] = lax.dot_general(k_ref[keys, head_lanes(h)], q_ref[:, head_lanes(h)], nt_dims,
                                       preferred_element_type=F32)
        for h in range(MLA_HEADS):
            if diagonal:
                acc_ref[h] = jnp.zeros((LANES, tq), F32)
            else:
                acc_ref[h] += jnp.dot(v_ref[j_prev, head_lanes(h), :], p_ref[h], preferred_element_type=F32)
        for h in range(MLA_HEADS):
            for c in range(tq // cw):
                cols = slice(c * cw, (c + 1) * cw)
                nk = min(tk, (c + 1) * cw) if diagonal else tk
                s = s_ref[h, :nk, cols]
                if diagonal:
                    s = jnp.where(causal[c][:nk], s, -jnp.inf)
                    m_new = jnp.max(s, axis=0, keepdims=True)
                    if nk < tk:
                        p_ref[h, nk:, cols] = jnp.zeros((tk - nk, cw), BF16)
                else:
                    m_old = m_ref[h, :, cols]
                    m_new = jnp.maximum(m_old, jnp.max(s, axis=0, keepdims=True))
                    acc_ref[h, :, cols] *= jnp.exp2(m_old - m_new)
                p_ref[h, :nk, cols] = jnp.exp2(s - m_new).astype(BF16)
                m_ref[h, :, cols] = m_new

    visit(i, i, diagonal=True)

    def body(j, carry):
        visit(j, jnp.where(j == 0, i, j - 1), diagonal=False)
        return carry

    lax.fori_loop(0, i, body, 0)
    j_last = jnp.where(i == 0, i, i - 1)
    for h in range(MLA_HEADS):
        acc_ref[h] += jnp.dot(v_ref[j_last, head_lanes(h), :], p_ref[h], preferred_element_type=F32)
    for p in range(HEAD_PAIRS):
        outs = [acc_ref[2 * p + e, :MLA_V] / acc_ref[2 * p + e, MLA_V:] for e in range(2)]
        o_ref[:, p * LANES:(p + 1) * LANES] = jnp.concatenate(outs, axis=0).T


def _mla_attention(q, k, v, B, S):
    tq = tk = ATT_TILE
    nq = S // tq
    width = MLA_HEADS * LANES
    return pl.pallas_call(
        _mla_kernel,
        grid=(B, nq),
        in_specs=[
            pl.BlockSpec((tq, width), lambda b, i: (b * nq + i, 0)),
            pl.BlockSpec((S, width), lambda b, i: (b, 0)),
            pl.BlockSpec((S // tk, width, tk), lambda b, i: (b, 0, 0)),
        ],
        out_specs=pl.BlockSpec((tq, MLA_WIDTH), lambda b, i: (b * nq + i, 0)),
        out_shape=jax.ShapeDtypeStruct((B * S, MLA_WIDTH), F32),
        scratch_shapes=[
            pltpu.VMEM((MLA_HEADS, tk, tq), F32),
            pltpu.VMEM((MLA_HEADS, tk, tq), BF16),
            pltpu.VMEM((MLA_HEADS, LANES, tq), F32),
            pltpu.VMEM((MLA_HEADS, 1, tq), F32),
        ],
        compiler_params=_cparams("parallel", "parallel"),
        name="mla_attn",
    )(q, k, v)


ROUTER_ROWS = SUBLANES * (1 + N_GROUPS)


def _split_bf16(a):
    hi = a.astype(BF16)
    return hi, (a - hi.astype(F32)).astype(BF16)


def _route_kernel(osb_ref, omla_ref, x_ref, g_sb_ref, g_mla_ref, wout_ref, fn_ref, wr_ref, br_ref,
                  x1_ref, hext_ref, bucket_ref, rank_ref, counts_ref, carry_ref):
    tm = ROW_TILE

    @pl.when(pl.program_id(0) == 0)
    def _():
        carry_ref[...] = jnp.zeros_like(carry_ref)

    part = tm // ROUTE_PARTS
    parts = [slice(p * part, (p + 1) * part) for p in range(ROUTE_PARTS)]
    x1s = []
    for rows in parts:
        o = jnp.concatenate([_rms(osb_ref[rows]) * g_sb_ref[...], _rms(omla_ref[rows]) * g_mla_ref[...]], axis=-1)
        x1 = x_ref[rows] + jnp.dot(o.astype(BF16), wout_ref[...], preferred_element_type=F32)
        x1_ref[rows] = x1
        x1s.append(x1)
    logit_parts = []
    for rows, x1 in zip(parts, x1s):
        h = _rms(x1) * fn_ref[...]
        hext_ref[rows, :D_MODEL] = h
        h_hi, h_lo = _split_bf16(h)
        both = jnp.dot(h_hi, wr_ref[...], preferred_element_type=F32)
        logit_parts.append((both[:, :LANES] + both[:, LANES:]
                            + jnp.dot(h_lo, wr_ref[:, :LANES], preferred_element_type=F32)) + br_ref[...])
    for p, (rows, logits) in enumerate(zip(parts, logit_parts)):
        _route_part(logits, rows, slice(p * part, (p + 1) * part), hext_ref, bucket_ref, rank_ref, carry_ref)
    counts_ref[...] = carry_ref[...]


def _route_part(logits, rows, cols, hext_ref, bucket_ref, rank_ref, carry_ref):
    tm = logits.shape[0]
    lt = logits.T

    rid = lax.broadcasted_iota(jnp.int32, (SUBLANES, tm), 0)
    g_logit = jnp.where(rid < N_GROUPS, lt[:SUBLANES], -jnp.inf)
    g_exp = jnp.exp(g_logit - jnp.max(g_logit, axis=0, keepdims=True))
    p_group = g_exp / jnp.sum(g_exp, axis=0, keepdims=True)
    g_val = jnp.max(p_group, axis=0, keepdims=True)
    g_idx = jnp.min(jnp.where(p_group == g_val, rid, SUBLANES), axis=0, keepdims=True)
    local = lt[SUBLANES * N_GROUPS:SUBLANES * (N_GROUPS + 1)]
    for g in range(N_GROUPS - 2, -1, -1):
        local = jnp.where(g_idx == g, lt[SUBLANES * (g + 1):SUBLANES * (g + 2)], local)
    e_exp = jnp.exp(local - jnp.max(local, axis=0, keepdims=True))
    p_exp = e_exp / jnp.sum(e_exp, axis=0, keepdims=True)
    v1 = jnp.max(p_exp, axis=0, keepdims=True)
    i1 = jnp.min(jnp.where(p_exp == v1, rid, SUBLANES), axis=0, keepdims=True)
    rest = jnp.where(rid == i1, -1.0, p_exp)
    v2 = jnp.max(rest, axis=0, keepdims=True)
    i2 = jnp.min(jnp.where(rest == v2, rid, SUBLANES), axis=0, keepdims=True)
    den = v1 + v2
    w1 = g_val * v1 / den
    w2 = g_val * v2 / den
    first_lower = i1 < i2
    e_lo = jnp.where(first_lower, i1, i2)
    e_hi = jnp.where(first_lower, i2, i1)
    w_lo = jnp.where(first_lower, w1, w2)
    w_hi = jnp.where(first_lower, w2, w1)
    pair = ((e_lo * (2 * EXPERTS_PER_GROUP - 1 - e_lo)) >> 1) + (e_hi - e_lo - 1)
    bucket = g_idx * N_PAIRS + pair
    bucket_ref[0, :, cols] = bucket

    rid_full = lax.broadcasted_iota(jnp.int32, (LANES, tm), 0)
    w_rows = jnp.where(rid_full == 0, w_lo, jnp.where(rid_full == 1, w_hi, 0.0))
    hext_ref[rows, D_MODEL:] = w_rows.T

    onehot = (rid_full == bucket).astype(F32)
    trow = lax.broadcasted_iota(jnp.int32, (tm, tm), 0)
    tcol = lax.broadcasted_iota(jnp.int32, (tm, tm), 1)
    earlier = (trow < tcol).astype(BF16)
    before = jnp.dot(onehot.astype(BF16), earlier, preferred_element_type=F32) + carry_ref[...]
    rank_ref[0, :, cols] = jnp.sum(onehot * before, axis=0, keepdims=True).astype(jnp.int32)
    carry_ref[...] += jnp.sum(onehot, axis=1, keepdims=True)


def _route(o_sb, o_mla, x2d, g_sb, g_mla, w_out, fn, wr, br):
    T = x2d.shape[0]
    tm = ROW_TILE
    nt = T // tm
    row = lambda n: pl.BlockSpec((tm, n), lambda i: (i, 0))
    full = lambda a: pl.BlockSpec(a.shape, lambda i: (0,) * a.ndim)
    tok = pl.BlockSpec((1, 1, tm), lambda i: (i, 0, 0))
    ins = [o_sb, o_mla, x2d, g_sb, g_mla, w_out, fn, wr, br]
    return pl.pallas_call(
        _route_kernel,
        grid=(nt,),
        in_specs=[row(SB_WIDTH), row(MLA_WIDTH), row(D_MODEL)] + [full(a) for a in ins[3:]],
        out_specs=[row(D_MODEL), row(EXT_WIDTH), tok, tok, pl.BlockSpec((LANES, 1), lambda i: (0, 0))],
        out_shape=[
            jax.ShapeDtypeStruct((T, D_MODEL), F32),
            jax.ShapeDtypeStruct((T, EXT_WIDTH), F32),
            jax.ShapeDtypeStruct((nt, 1, tm), jnp.int32),
            jax.ShapeDtypeStruct((nt, 1, tm), jnp.int32),
            jax.ShapeDtypeStruct((LANES, 1), F32),
        ],
        scratch_shapes=[pltpu.VMEM((LANES, 1), F32)],
        compiler_params=_cparams("arbitrary"),
        name="route",
    )(*ins)


DEST_TILES = 8


def _dest_kernel(bucket_ref, rank_ref, offs_ref, dest_ref):
    tm = bucket_ref.shape[-1]
    rid = lax.broadcasted_iota(jnp.int32, (LANES, tm), 0)
    for t in range(bucket_ref.shape[0]):
        start = jnp.sum(jnp.where(rid == bucket_ref[t], offs_ref[...], 0), axis=0, keepdims=True)
        dest_ref[t] = start + rank_ref[t]


def _dest_rows(bucket, rank, offsets):
    nt, _, tm = bucket.shape
    assert nt % DEST_TILES == 0
    tok = pl.BlockSpec((DEST_TILES, 1, tm), lambda i: (i, 0, 0))
    return pl.pallas_call(
        _dest_kernel,
        grid=(nt // DEST_TILES,),
        in_specs=[tok, tok, pl.BlockSpec((LANES, 1), lambda i: (0, 0))],
        out_specs=tok,
        out_shape=jax.ShapeDtypeStruct((nt, 1, tm), jnp.int32),
        compiler_params=_cparams("parallel"),
        name="dest",
    )(bucket, rank, offsets)


def _scatter_kernel(dest_ref, h_ref, init_ref, xs_ref, sem):
    del init_ref
    tm = h_ref.shape[0]
    base = pl.program_id(0) * tm

    for r in range(tm):
        pltpu.make_async_copy(h_ref.at[pl.ds(r, 1)], xs_ref.at[pl.ds(dest_ref[base + r], 1)], sem).start()
    pltpu.make_async_copy(h_ref, xs_ref.at[pl.ds(0, tm)], sem).wait()


def _scatter_rows(dest, hext, n_rows):
    T, W = hext.shape
    tm = DISPATCH_TILE
    init = jnp.zeros((n_rows, W), hext.dtype)
    return pl.pallas_call(
        _scatter_kernel,
        grid_spec=pltpu.PrefetchScalarGridSpec(
            num_scalar_prefetch=1,
            grid=(T // tm,),
            in_specs=[pl.BlockSpec((tm, W), lambda i, d: (i, 0)), pl.BlockSpec(memory_space=pl.ANY)],
            out_specs=pl.BlockSpec(memory_space=pl.ANY),
            scratch_shapes=[pltpu.SemaphoreType.DMA],
        ),
        out_shape=jax.ShapeDtypeStruct((n_rows, W), hext.dtype),
        input_output_aliases={2: 0},
        compiler_params=_cparams("arbitrary"),
        name="scatter",
    )(dest, hext, init)


def _moe_kernel(elo_ref, ehi_ref, nt_ref, xs_ref, wgu_lo_ref, wd_lo_ref, wgu_hi_ref, wd_hi_ref, ys_ref):
    del elo_ref, ehi_ref
    used = pl.program_id(0) < nt_ref[0]

    @pl.when(jnp.logical_not(used))
    def _():
        ys_ref[...] = jnp.zeros_like(ys_ref)

    @pl.when(used)
    def _():
        h = xs_ref[:, :D_MODEL].astype(BF16)
        gates = xs_ref[:, D_MODEL:]
        gu_lo = jnp.dot(h, wgu_lo_ref[0], preferred_element_type=F32)
        gu_hi = jnp.dot(h, wgu_hi_ref[0], preferred_element_type=F32)

        def down(gu, wd_ref):
            g, u = gu[:, :D_EXPERT], gu[:, D_EXPERT:]
            hid = (g * jax.nn.sigmoid(g)) * u
            return jnp.dot(hid.astype(BF16), wd_ref[0], preferred_element_type=F32)

        ys_ref[...] = down(gu_lo, wd_lo_ref) * gates[:, 0:1] + down(gu_hi, wd_hi_ref) * gates[:, 1:2]


def _moe(tile_elo, tile_ehi, n_tiles_used, xs, w_gu, w_d):
    n_rows = xs.shape[0]
    tile = MOE_TILE
    rows = lambda w: pl.BlockSpec((tile, w), lambda i, elo, ehi, nt: (jnp.minimum(i, nt[0] - 1), 0))
    wspec = lambda shape, which: pl.BlockSpec(
        (1,) + shape, lambda i, elo, ehi, nt: ((elo, ehi)[which][i], 0, 0))
    return pl.pallas_call(
        _moe_kernel,
        grid_spec=pltpu.PrefetchScalarGridSpec(
            num_scalar_prefetch=3,
            grid=(n_rows // tile,),
            in_specs=[
                rows(EXT_WIDTH),
                wspec((D_MODEL, 2 * D_EXPERT), 0), wspec((D_EXPERT, D_MODEL), 0),
                wspec((D_MODEL, 2 * D_EXPERT), 1), wspec((D_EXPERT, D_MODEL), 1),
            ],
            out_specs=pl.BlockSpec((tile, D_MODEL), lambda i, elo, ehi, nt: (i, 0)),
        ),
        out_shape=jax.ShapeDtypeStruct((n_rows, D_MODEL), F32),
        compiler_params=_cparams("arbitrary"),
        name="moe",
    )(tile_elo, tile_ehi, n_tiles_used, xs, w_gu, w_d, w_gu, w_d)


def _final_kernel(dest_ref, ys_ref, x1_ref, fn_ref, o_ref, buf_ref, sem):
    tm = x1_ref.shape[0]
    i = pl.program_id(0)
    slot = i % 2

    def start_gather(t, s):
        for r in range(tm):
            pltpu.make_async_copy(ys_ref.at[pl.ds(dest_ref[t * tm + r], 1)], buf_ref.at[s, pl.ds(r, 1)],
                                  sem.at[s]).start()

    @pl.when(i == 0)
    def _():
        start_gather(0, 0)

    @pl.when(i + 1 < pl.num_programs(0))
    def _():
        start_gather(i + 1, 1 - slot)

    pltpu.make_async_copy(ys_ref.at[pl.ds(0, tm)], buf_ref.at[slot], sem.at[slot]).wait()
    o_ref[...] = _rms(x1_ref[...] + buf_ref[slot]) * fn_ref[...]


def _final(dest, ys, x1, fn):
    T, D = x1.shape
    tm = DISPATCH_TILE
    return pl.pallas_call(
        _final_kernel,
        grid_spec=pltpu.PrefetchScalarGridSpec(
            num_scalar_prefetch=1,
            grid=(T // tm,),
            in_specs=[
                pl.BlockSpec(memory_space=pl.ANY),
                pl.BlockSpec((tm, D), lambda i, d: (i, 0)),
                pl.BlockSpec((1, D), lambda i, d: (0, 0)),
            ],
            out_specs=pl.BlockSpec((tm, D), lambda i, d: (i, 0)),
            scratch_shapes=[pltpu.VMEM((2, tm, D), F32), pltpu.SemaphoreType.DMA((2,))],
        ),
        out_shape=jax.ShapeDtypeStruct((T, D), F32),
        compiler_params=_cparams("arbitrary"),
        name="final",
    )(dest, ys, x1, fn)


def _pair_tables():
    lo, hi = [], []
    for g in range(N_GROUPS):
        for a in range(EXPERTS_PER_GROUP):
            for b in range(a + 1, EXPERTS_PER_GROUP):
                lo.append(g * EXPERTS_PER_GROUP + a)
                hi.append(g * EXPERTS_PER_GROUP + b)
    return np.asarray(lo, np.int32), np.asarray(hi, np.int32)


def _attention_weights(w_in, q_norm, w_uq, kv_norm, w_ukv):
    D = w_in.shape[0]
    c0 = 3 * SB_WIDTH
    zeros = lambda r, c: jnp.zeros((r, c), F32)
    w_kr = jnp.concatenate(
        [zeros(D, MLA_NOPE), w_in[:, c0 + Q_LORA + KV_LORA:], zeros(D, LANES - MLA_NOPE - MLA_ROPE)], axis=1)
    dq = MLA_NOPE + MLA_ROPE
    uq = jnp.concatenate(
        [jnp.concatenate([w_uq[:, h * dq:(h + 1) * dq], zeros(Q_LORA, LANES - dq)], axis=1)
         for h in range(MLA_HEADS)], axis=1)
    dkv = MLA_NOPE + MLA_V
    uk = jnp.concatenate(
        [jnp.concatenate([w_ukv[:, h * dkv:h * dkv + MLA_NOPE], zeros(KV_LORA, LANES - MLA_NOPE)], axis=1)
         for h in range(MLA_HEADS)], axis=1)
    uv = jnp.concatenate(
        [jnp.concatenate([w_ukv[:, h * dkv + MLA_NOPE:(h + 1) * dkv].T, zeros(LANES - MLA_V, KV_LORA)], axis=0)
         for h in range(MLA_HEADS)], axis=0)
    return {
        "sb": w_in[:, :2 * SB_WIDTH].astype(BF16),
        "vsb": w_in[:, 2 * SB_WIDTH:c0].T.astype(BF16),
        "cq": w_in[:, c0:c0 + Q_LORA].astype(BF16),
        "ckv": w_in[:, c0 + Q_LORA:c0 + Q_LORA + KV_LORA].astype(BF16),
        "kr": w_kr.astype(BF16),
        "qn": q_norm[None, :],
        "uq": uq.astype(BF16),
        "kvn": kv_norm[None, :],
        "uk": uk.astype(BF16),
        "uv": uv.astype(BF16),
    }


def _router_weights(w_group, b_group, w_expert, b_expert):
    D = w_group.shape[0]
    pad_g = SUBLANES - N_GROUPS
    pad_e = LANES - SUBLANES - N_EXPERTS
    w = jnp.concatenate([w_group, jnp.zeros((D, pad_g), F32), w_expert, jnp.zeros((D, pad_e), F32)], axis=1)
    b = jnp.concatenate([b_group, jnp.zeros((pad_g,), F32), b_expert, jnp.zeros((pad_e,), F32)])[None, :]
    return jnp.concatenate(_split_bf16(w), axis=1), b


def _bucket_layout(counts, n_tiles):
    c = counts[:N_BUCKETS, 0].astype(jnp.int32)
    tiles = (c + MOE_TILE - 1) // MOE_TILE
    tile_end = jnp.cumsum(tiles)
    starts = (tile_end - tiles) * MOE_TILE
    offsets = jnp.concatenate([starts, jnp.zeros((LANES - N_BUCKETS,), jnp.int32)])[:, None]
    n_used = tile_end[-1]
    tile_id = jnp.minimum(jnp.arange(n_tiles, dtype=jnp.int32), n_used - 1)
    tile_bucket = jnp.sum(tile_id[:, None] >= tile_end[None, :], axis=1)
    onehot = tile_bucket[:, None] == jnp.arange(N_BUCKETS)[None, :]
    pick = lambda table: jnp.sum(jnp.where(onehot, table[None, :], 0), axis=1).astype(jnp.int32)
    pair_lo, pair_hi = _pair_tables()
    return offsets, pick(jnp.asarray(pair_lo)), pick(jnp.asarray(pair_hi)), n_used[None]


def kernel(x, positions, attn_norm, w_in, q_norm, w_uq, kv_norm, w_ukv, sb_out_norm, mla_out_norm, w_out,
           ffn_norm, w_group_router, b_group_router, w_expert_router, b_expert_router, w_gate, w_up, w_down,
           final_norm):
    B, S, D = x.shape
    T = B * S
    depth = w_in.shape[0]
    assert D == D_MODEL and T % ROW_TILE == 0 and T % DISPATCH_TILE == 0 and S % ATT_TILE == 0
    assert depth == 1, "the final norm is fused into the last layer's gather kernel"
    n_sorted_tiles = T // MOE_TILE + N_BUCKETS
    n_sorted_rows = n_sorted_tiles * MOE_TILE

    lane = jnp.arange(LANES)
    invf = (ROPE_BASE ** (-(lane % ROPE_HALF).astype(F32) / ROPE_HALF))[None, :]
    pos2d = positions.reshape(T, 1)
    x2d = x.reshape(T, D)
    for l in range(depth):
        aw = _attention_weights(w_in[l], q_norm[l], w_uq[l], kv_norm[l], w_ukv[l])
        q_sb, k_sb, v_sb, q_m, k_m, v_m = _projections(x2d, pos2d, invf, attn_norm[l][None, :], aw)
        o_sb = _sb_attention(q_sb, k_sb, v_sb, B, S)
        o_mla = _mla_attention(q_m, k_m, v_m, B, S)

        wr, br = _router_weights(w_group_router[l], b_group_router[l], w_expert_router[l], b_expert_router[l])
        x1, hext, bucket, rank, counts = _route(
            o_sb, o_mla, x2d, sb_out_norm[l][None, :], mla_out_norm[l][None, :], w_out[l].astype(BF16),
            ffn_norm[l][None, :], wr, br)
        offsets, tile_elo, tile_ehi, n_used = _bucket_layout(counts, n_sorted_tiles)
        dest = _dest_rows(bucket, rank, offsets).reshape(T)

        xs = _scatter_rows(dest, hext, n_sorted_rows)
        w_gu = jnp.concatenate([w_gate[l], w_up[l]], axis=-1).astype(BF16)
        ys = _moe(tile_elo, tile_ehi, n_used, xs, w_gu, w_down[l].astype(BF16))
        x2d = _final(dest, ys, x1, final_norm[None, :])
    return x2d.reshape(B, S, D)
```

```python
import functools

import jax
import jax.numpy as jnp
import numpy as np
from jax import lax
from jax.experimental import pallas as pl
from jax.experimental.pallas import tpu as pltpu

F32 = jnp.float32
BF16 = jnp.bfloat16

D_MODEL = 1024
SB_HEADS = 8
SB_HEAD_DIM = 64
SB_WIDTH = SB_HEADS * SB_HEAD_DIM
MLA_HEADS = 8
MLA_NOPE = 64
MLA_ROPE = 32
MLA_V = 64
MLA_WIDTH = MLA_HEADS * MLA_V
Q_LORA = 256
KV_LORA = 128
ROPE_BASE = 10000.0
N_GROUPS = 4
EXPERTS_PER_GROUP = 8
N_EXPERTS = N_GROUPS * EXPERTS_PER_GROUP
D_EXPERT = 256
EPS = 1e-6
LOG2_E = 1.4426950408889634

LANES = 128
SUBLANES = 8
N_PAIRS = EXPERTS_PER_GROUP * (EXPERTS_PER_GROUP - 1) // 2
N_BUCKETS = N_GROUPS * N_PAIRS
assert N_BUCKETS <= LANES
ROPE_HALF = MLA_ROPE // 2
HEAD_PAIRS = SB_HEADS // 2
assert SB_HEADS == MLA_HEADS and 2 * SB_HEAD_DIM == LANES and 2 * MLA_V == LANES

ROW_TILE = 512
DISPATCH_TILE = 1024
ATT_TILE = 256
ATT_CHUNK = LANES
ROUTE_PARTS = 1
MOE_TILE = 256
EXT_WIDTH = D_MODEL + LANES
SB_UNDERFLOW = 151.0
VMEM_LIMIT = 56 * 1024 * 1024


def _rms(x):
    return x * lax.rsqrt(jnp.mean(x * x, axis=-1, keepdims=True) + EPS)


def _cparams(*sem):
    return pltpu.CompilerParams(dimension_semantics=sem, vmem_limit_bytes=VMEM_LIMIT)


def _proj_kernel(x_ref, pos_ref, invf_ref, an_ref, wsb_ref, wvsb_ref, wcq_ref, wckv_ref, wkr_ref, qn_ref, wuq_ref,
                 kvn_ref, wuk_ref, wuv_ref, qsb_ref, ksb_ref, vsb_ref, qm_ref, km_ref, vm_ref):
    tk = ATT_TILE
    nt_dims = (((1,), (1,)), ((), ()))
    hb = (_rms(x_ref[...]) * an_ref[...]).astype(BF16)
    cq = jnp.dot(hb, wcq_ref[...], preferred_element_type=F32)
    ckv = jnp.dot(hb, wckv_ref[...], preferred_element_type=F32)
    kr = jnp.dot(hb, wkr_ref[...], preferred_element_type=F32)
    sb = jnp.dot(hb, wsb_ref[...], preferred_element_type=F32)
    qsb_ref[...] = (sb[:, :SB_WIDTH] * (SB_HEAD_DIM ** -0.5 * LOG2_E)).astype(BF16)
    ksb_ref[...] = sb[:, SB_WIDTH:].astype(BF16)
    v_sb = lax.dot_general(wvsb_ref[...], hb, nt_dims, preferred_element_type=F32).astype(BF16)
    for kb in range(v_sb.shape[1] // tk):
        vsb_ref[kb] = v_sb[:, kb * tk:(kb + 1) * tk]

    q = jnp.dot((_rms(cq) * qn_ref[...]).astype(BF16), wuq_ref[...], preferred_element_type=F32)
    ckn = (_rms(ckv) * kvn_ref[...]).astype(BF16)
    kn = jnp.dot(ckn, wuk_ref[...], preferred_element_type=F32)
    v_m = lax.dot_general(wuv_ref[...], ckn, nt_dims, preferred_element_type=F32)
    ones_rows = (lax.broadcasted_iota(jnp.int32, (MLA_HEADS * LANES, 1), 0) % LANES) >= MLA_V
    v_m = jnp.where(ones_rows, 1.0, v_m).astype(BF16)
    for kb in range(v_m.shape[1] // tk):
        vm_ref[kb] = v_m[:, kb * tk:(kb + 1) * tk]

    ang = pos_ref[...].astype(F32) * invf_ref[...]
    cos, sin = jnp.cos(ang), jnp.sin(ang)
    lane = lax.broadcasted_iota(jnp.int32, (1, LANES), 1)
    x1_lanes = (lane >= MLA_NOPE) & (lane < MLA_NOPE + ROPE_HALF)
    x2_lanes = (lane >= MLA_NOPE + ROPE_HALF) & (lane < MLA_NOPE + MLA_ROPE)
    c_tab = jnp.where(lane < MLA_NOPE, 1.0, jnp.where(x1_lanes | x2_lanes, cos, 0.0))
    s_from_x2 = jnp.where(x1_lanes, -sin, 0.0)
    s_from_x1 = jnp.where(x2_lanes, sin, 0.0)

    def rope(t):
        return (t * c_tab + pltpu.roll(t, LANES - ROPE_HALF, 1) * s_from_x2
                + pltpu.roll(t, ROPE_HALF, 1) * s_from_x1)

    k_rope = rope(kr)
    q_scale = (MLA_NOPE + MLA_ROPE) ** -0.5 * LOG2_E
    for h in range(MLA_HEADS):
        blk = slice(h * LANES, (h + 1) * LANES)
        qm_ref[:, blk] = (rope(q[:, blk]) * q_scale).astype(BF16)
        km_ref[:, blk] = (kn[:, blk] + k_rope).astype(BF16)


def _projections(x2d, pos2d, invf, an, w):
    T = x2d.shape[0]
    tm = ROW_TILE
    tk = ATT_TILE
    row = lambda n: pl.BlockSpec((tm, n), lambda i: (i, 0))
    slab = lambda n: pl.BlockSpec((tm // tk, n, tk), lambda i: (i, 0, 0))
    full = lambda a: pl.BlockSpec(a.shape, lambda i: (0,) * a.ndim)
    ins = [x2d, pos2d, invf, an, w["sb"], w["vsb"], w["cq"], w["ckv"], w["kr"], w["qn"], w["uq"], w["kvn"],
           w["uk"], w["uv"]]
    in_specs = [row(D_MODEL), row(1)] + [full(a) for a in ins[2:]]
    wide = MLA_HEADS * LANES
    row_out = lambda n: jax.ShapeDtypeStruct((T, n), BF16)
    slab_out = lambda n: jax.ShapeDtypeStruct((T // tk, n, tk), BF16)
    return pl.pallas_call(
        _proj_kernel,
        grid=(T // tm,),
        in_specs=in_specs,
        out_specs=[row(SB_WIDTH), row(SB_WIDTH), slab(SB_WIDTH), row(wide), row(wide), slab(wide)],
        out_shape=[row_out(SB_WIDTH), row_out(SB_WIDTH), slab_out(SB_WIDTH), row_out(wide), row_out(wide),
                   slab_out(wide)],
        compiler_params=_cparams("parallel"),
        name="proj",
    )(*ins)


def _softplus2(z):
    sign_bit = jnp.uint32(0x80000000)
    neg_abs = lax.bitcast_convert_type(lax.bitcast_convert_type(z, jnp.uint32) | sign_bit, F32)
    return jnp.maximum(z, 0.0) + jnp.log2(1.0 + jnp.exp2(neg_abs))


def _sb_kernel(q_ref, k_ref, v_ref, o_ref, z_ref, w_ref, acc_ref, c_ref, scale_ref):
    tq = tk = ATT_TILE
    cw = ATT_CHUNK
    i = pl.program_id(1)
    lane = lax.broadcasted_iota(jnp.int32, (1, LANES), 1)
    in_head = (lane < SB_HEAD_DIM, lane >= SB_HEAD_DIM)
    later = (lax.broadcasted_iota(jnp.int32, (tk, tk), 1)
             > lax.broadcasted_iota(jnp.int32, (tk, tk), 0)).astype(BF16)
    key_idx = lax.broadcasted_iota(jnp.int32, (tk, cw), 0)
    qry_idx = lax.broadcasted_iota(jnp.int32, (tk, cw), 1)
    strict = [key_idx < qry_idx + c * cw for c in range(tq // cw)]
    nt_dims = (((1,), (1,)), ((), ()))
    pair_lanes = lambda h: slice((h // 2) * LANES, (h // 2 + 1) * LANES)

    def add_values(j_blk, h):
        v_head = v_ref[j_blk, h * SB_HEAD_DIM:(h + 1) * SB_HEAD_DIM, :]
        acc_ref[h] += jnp.dot(v_head, w_ref[h], preferred_element_type=F32) * scale_ref[h]

    def visit(j, j_prev, diagonal):
        keys = pl.ds(pl.multiple_of(j * tk, tk), tk)
        for h in range(SB_HEADS):
            q_pair = q_ref[:, pair_lanes(h)]
            qh = jnp.where(in_head[h % 2], q_pair, jnp.zeros_like(q_pair))
            z_ref[h] = lax.dot_general(k_ref[keys, pair_lanes(h)], qh, nt_dims, preferred_element_type=F32)
            if diagonal:
                acc_ref[h] = jnp.zeros((SB_HEAD_DIM, tq), F32)
            else:
                add_values(j_prev, h)
        for h in range(SB_HEADS):
            for c in range(tq // cw):
                cols = slice(c * cw, (c + 1) * cw)
                nk = min(tk, (c + 1) * cw) if diagonal else tk
                z = z_ref[h, :nk, cols]
                sp = _softplus2(z)
                if diagonal:
                    sp = jnp.where(strict[c][:nk], sp, 0.0)
                    if nk < tk:
                        w_ref[h, nk:, cols] = jnp.zeros((tk - nk, cw), BF16)
                z_ref[h, :nk, cols] = z - sp
                w_ref[h, :nk, cols] = sp.astype(BF16)
        c_low = None
        for h in range(SB_HEADS):
            suffix = jnp.dot(later, w_ref[h], preferred_element_type=F32)
            block_sum = suffix[0:1, :] + w_ref[h, 0:1, :].astype(F32)
            if diagonal:
                scale_ref[h] = jnp.ones((1, tq), F32)
                c_new = block_sum
            else:
                c_old = c_ref[h]
                scale_ref[h] = jnp.exp2(-c_old)
                c_new = c_old + block_sum
            c_ref[h] = c_new
            c_low = c_new if c_low is None else jnp.minimum(c_low, c_new)
            for c in range(tq // cw):
                cols = slice(c * cw, (c + 1) * cw)
                nk = min(tk, (c + 1) * cw) if diagonal else tk
                a = jnp.exp2(z_ref[h, :nk, cols] - suffix[:nk, cols])
                if diagonal:
                    a = jnp.where(strict[c][:nk], a, 0.0)
                w_ref[h, :nk, cols] = a.astype(BF16)
        return jnp.min(c_low)

    def cond(carry):
        j, _, c_min = carry
        return (j >= 0) & (c_min < SB_UNDERFLOW)

    def body(carry):
        j, j_prev, _ = carry
        return j - 1, j, visit(j, j_prev, diagonal=False)

    _, j_last, _ = lax.while_loop(cond, body, (i - 1, i, visit(i, i, diagonal=True)))
    for h in range(SB_HEADS):
        add_values(j_last, h)
    for p in range(HEAD_PAIRS):
        o_ref[:, p * LANES:(p + 1) * LANES] = jnp.concatenate([acc_ref[2 * p], acc_ref[2 * p + 1]], axis=0).T


def _sb_attention(q, k, v, B, S):
    tq = tk = ATT_TILE
    nq = S // tq
    return pl.pallas_call(
        _sb_kernel,
        grid=(B, nq),
        in_specs=[
            pl.BlockSpec((tq, SB_WIDTH), lambda b, i: (b * nq + i, 0)),
            pl.BlockSpec((S, SB_WIDTH), lambda b, i: (b, 0)),
            pl.BlockSpec((S // tk, SB_WIDTH, tk), lambda b, i: (b, 0, 0)),
        ],
        out_specs=pl.BlockSpec((tq, SB_WIDTH), lambda b, i: (b * nq + i, 0)),
        out_shape=jax.ShapeDtypeStruct((B * S, SB_WIDTH), F32),
        scratch_shapes=[
            pltpu.VMEM((SB_HEADS, tk, tq), F32),
            pltpu.VMEM((SB_HEADS, tk, tq), BF16),
            pltpu.VMEM((SB_HEADS, SB_HEAD_DIM, tq), F32),
            pltpu.VMEM((SB_HEADS, 1, tq), F32),
            pltpu.VMEM((SB_HEADS, 1, tq), F32),
        ],
        compiler_params=_cparams("parallel", "parallel"),
        name="sb_attn",
    )(q, k, v)


def _mla_kernel(q_ref, k_ref, v_ref, o_ref, s_ref, p_ref, acc_ref, m_ref):
    tq = tk = ATT_TILE
    cw = ATT_CHUNK
    i = pl.program_id(1)
    key_idx = lax.broadcasted_iota(jnp.int32, (tk, cw), 0)
    qry_idx = lax.broadcasted_iota(jnp.int32, (tk, cw), 1)
    causal = [key_idx <= qry_idx + c * cw for c in range(tq // cw)]
    nt_dims = (((1,), (1,)), ((), ()))
    head_lanes = lambda h: slice(h * LANES, (h + 1) * LANES)

    def visit(j, j_prev, diagonal):
        keys = pl.ds(pl.multiple_of(j * tk, tk), tk)

        for h in range(MLA_HEADS):
            s_ref[h] = lax.dot_general(k_ref[keys, head_lanes(h)], q_ref[:, head_lanes(h)], nt_dims,
                                       preferred_element_type=F32)
        for h in range(MLA_HEADS):
            if diagonal:
                acc_ref[h] = jnp.zeros((LANES, tq), F32)
            else:
                acc_ref[h] += jnp.dot(v_ref[j_prev, head_lanes(h), :], p_ref[h], preferred_element_type=F32)
        for h in range(MLA_HEADS):
            for c in range(tq // cw):
                cols = slice(c * cw, (c + 1) * cw)
                nk = min(tk, (c + 1) * cw) if diagonal else tk
                s = s_ref[h, :nk, cols]
                if diagonal:
                    s = jnp.where(causal[c][:nk], s, -jnp.inf)
                    m_new = jnp.max(s, axis=0, keepdims=True)
                    if nk < tk:
                        p_ref[h, nk:, cols] = jnp.zeros((tk - nk, cw), BF16)
                else:
                    m_old = m_ref[h, :, cols]
                    m_new = jnp.maximum(m_old, jnp.max(s, axis=0, keepdims=True))
                    acc_ref[h, :, cols] *= jnp.exp2(m_old - m_new)
                p_ref[h, :nk, cols] = jnp.exp2(s - m_new).astype(BF16)
                m_ref[h, :, cols] = m_new

    visit(i, i, diagonal=True)

    def body(j, carry):
        visit(j, jnp.where(j == 0, i, j - 1), diagonal=False)
        return carry

    lax.fori_loop(0, i, body, 0)
    j_last = jnp.where(i == 0, i, i - 1)
    for h in range(MLA_HEADS):
        acc_ref[h] += jnp.dot(v_ref[j_last, head_lanes(h), :], p_ref[h], preferred_element_type=F32)
    for p in range(HEAD_PAIRS):
        outs = [acc_ref[2 * p + e, :MLA_V] / acc_ref[2 * p + e, MLA_V:] for e in range(2)]
        o_ref[:, p * LANES:(p + 1) * LANES] = jnp.concatenate(outs, axis=0).T


def _mla_attention(q, k, v, B, S):
    tq = tk = ATT_TILE
    nq = S // tq
    width = MLA_HEADS * LANES
    return pl.pallas_call(
        _mla_kernel,
        grid=(B, nq),
        in_specs=[
            pl.BlockSpec((tq, width), lambda b, i: (b * nq + i, 0)),
            pl.BlockSpec((S, width), lambda b, i: (b, 0)),
            pl.BlockSpec((S // tk, width, tk), lambda b, i: (b, 0, 0)),
        ],
        out_specs=pl.BlockSpec((tq, MLA_WIDTH), lambda b, i: (b * nq + i, 0)),
        out_shape=jax.ShapeDtypeStruct((B * S, MLA_WIDTH), F32),
        scratch_shapes=[
            pltpu.VMEM((MLA_HEADS, tk, tq), F32),
            pltpu.VMEM((MLA_HEADS, tk, tq), BF16),
            pltpu.VMEM((MLA_HEADS, LANES, tq), F32),
            pltpu.VMEM((MLA_HEADS, 1, tq), F32),
        ],
        compiler_params=_cparams("parallel", "parallel"),
        name="mla_attn",
    )(q, k, v)


ROUTER_ROWS = SUBLANES * (1 + N_GROUPS)


def _split_bf16(a):
    hi = a.astype(BF16)
    return hi, (a - hi.astype(F32)).astype(BF16)


def _route_kernel(osb_ref, omla_ref, x_ref, g_sb_ref, g_mla_ref, wout_ref, fn_ref, wr_ref, br_ref,
                  x1_ref, hext_ref, bucket_ref, rank_ref, counts_ref, carry_ref):
    tm = ROW_TILE

    @pl.when(pl.program_id(0) == 0)
    def _():
        carry_ref[...] = jnp.zeros_like(carry_ref)

    part = tm // ROUTE_PARTS
    parts = [slice(p * part, (p + 1) * part) for p in range(ROUTE_PARTS)]
    x1s = []
    for rows in parts:
        o = jnp.concatenate([_rms(osb_ref[rows]) * g_sb_ref[...], _rms(omla_ref[rows]) * g_mla_ref[...]], axis=-1)
        x1 = x_ref[rows] + jnp.dot(o.astype(BF16), wout_ref[...], preferred_element_type=F32)
        x1_ref[rows] = x1
        x1s.append(x1)
    logit_parts = []
    for rows, x1 in zip(parts, x1s):
        h = _rms(x1) * fn_ref[...]
        hext_ref[rows, :D_MODEL] = h
        h_hi, h_lo = _split_bf16(h)
        both = jnp.dot(h_hi, wr_ref[...], preferred_element_type=F32)
        logit_parts.append((both[:, :LANES] + both[:, LANES:]
                            + jnp.dot(h_lo, wr_ref[:, :LANES], preferred_element_type=F32)) + br_ref[...])
    for p, (rows, logits) in enumerate(zip(parts, logit_parts)):
        _route_part(logits, rows, slice(p * part, (p + 1) * part), hext_ref, bucket_ref, rank_ref, carry_ref)
    counts_ref[...] = carry_ref[...]


def _route_part(logits, rows, cols, hext_ref, bucket_ref, rank_ref, carry_ref):
    tm = logits.shape[0]
    lt = logits.T

    rid = lax.broadcasted_iota(jnp.int32, (SUBLANES, tm), 0)
    g_logit = jnp.where(rid < N_GROUPS, lt[:SUBLANES], -jnp.inf)
    g_exp = jnp.exp(g_logit - jnp.max(g_logit, axis=0, keepdims=True))
    p_group = g_exp / jnp.sum(g_exp, axis=0, keepdims=True)
    g_val = jnp.max(p_group, axis=0, keepdims=True)
    g_idx = jnp.min(jnp.where(p_group == g_val, rid, SUBLANES), axis=0, keepdims=True)
    local = lt[SUBLANES * N_GROUPS:SUBLANES * (N_GROUPS + 1)]
    for g in range(N_GROUPS - 2, -1, -1):
        local = jnp.where(g_idx == g, lt[SUBLANES * (g + 1):SUBLANES * (g + 2)], local)
    e_exp = jnp.exp(local - jnp.max(local, axis=0, keepdims=True))
    p_exp = e_exp / jnp.sum(e_exp, axis=0, keepdims=True)
    v1 = jnp.max(p_exp, axis=0, keepdims=True)
    i1 = jnp.min(jnp.where(p_exp == v1, rid, SUBLANES), axis=0, keepdims=True)
    rest = jnp.where(rid == i1, -1.0, p_exp)
    v2 = jnp.max(rest, axis=0, keepdims=True)
    i2 = jnp.min(jnp.where(rest == v2, rid, SUBLANES), axis=0, keepdims=True)
    den = v1 + v2
    w1 = g_val * v1 / den
    w2 = g_val * v2 / den
    first_lower = i1 < i2
    e_lo = jnp.where(first_lower, i1, i2)
    e_hi = jnp.where(first_lower, i2, i1)
    w_lo = jnp.where(first_lower, w1, w2)
    w_hi = jnp.where(first_lower, w2, w1)
    pair = ((e_lo * (2 * EXPERTS_PER_GROUP - 1 - e_lo)) >> 1) + (e_hi - e_lo - 1)
    bucket = g_idx * N_PAIRS + pair
    bucket_ref[0, :, cols] = bucket

    rid_full = lax.broadcasted_iota(jnp.int32, (LANES, tm), 0)
    w_rows = jnp.where(rid_full == 0, w_lo, jnp.where(rid_full == 1, w_hi, 0.0))
    hext_ref[rows, D_MODEL:] = w_rows.T

    onehot = (rid_full == bucket).astype(F32)
    trow = lax.broadcasted_iota(jnp.int32, (tm, tm), 0)
    tcol = lax.broadcasted_iota(jnp.int32, (tm, tm), 1)
    earlier = (trow < tcol).astype(BF16)
    before = jnp.dot(onehot.astype(BF16), earlier, preferred_element_type=F32) + carry_ref[...]
    rank_ref[0, :, cols] = jnp.sum(onehot * before, axis=0, keepdims=True).astype(jnp.int32)
    carry_ref[...] += jnp.sum(onehot, axis=1, keepdims=True)


def _route(o_sb, o_mla, x2d, g_sb, g_mla, w_out, fn, wr, br):
    T = x2d.shape[0]
    tm = ROW_TILE
    nt = T // tm
    row = lambda n: pl.BlockSpec((tm, n), lambda i: (i, 0))
    full = lambda a: pl.BlockSpec(a.shape, lambda i: (0,) * a.ndim)
    tok = pl.BlockSpec((1, 1, tm), lambda i: (i, 0, 0))
    ins = [o_sb, o_mla, x2d, g_sb, g_mla, w_out, fn, wr, br]
    return pl.pallas_call(
        _route_kernel,
        grid=(nt,),
        in_specs=[row(SB_WIDTH), row(MLA_WIDTH), row(D_MODEL)] + [full(a) for a in ins[3:]],
        out_specs=[row(D_MODEL), row(EXT_WIDTH), tok, tok, pl.BlockSpec((LANES, 1), lambda i: (0, 0))],
        out_shape=[
            jax.ShapeDtypeStruct((T, D_MODEL), F32),
            jax.ShapeDtypeStruct((T, EXT_WIDTH), F32),
            jax.ShapeDtypeStruct((nt, 1, tm), jnp.int32),
            jax.ShapeDtypeStruct((nt, 1, tm), jnp.int32),
            jax.ShapeDtypeStruct((LANES, 1), F32),
        ],
        scratch_shapes=[pltpu.VMEM((LANES, 1), F32)],
        compiler_params=_cparams("arbitrary"),
        name="route",
    )(*ins)


DEST_TILES = 8


def _dest_kernel(bucket_ref, rank_ref, offs_ref, dest_ref):
    tm = bucket_ref.shape[-1]
    rid = lax.broadcasted_iota(jnp.int32, (LANES, tm), 0)
    for t in range(bucket_ref.shape[0]):
        start = jnp.sum(jnp.where(rid == bucket_ref[t], offs_ref[...], 0), axis=0, keepdims=True)
        dest_ref[t] = start + rank_ref[t]


def _dest_rows(bucket, rank, offsets):
    nt, _, tm = bucket.shape
    assert nt % DEST_TILES == 0
    tok = pl.BlockSpec((DEST_TILES, 1, tm), lambda i: (i, 0, 0))
    return pl.pallas_call(
        _dest_kernel,
        grid=(nt // DEST_TILES,),
        in_specs=[tok, tok, pl.BlockSpec((LANES, 1), lambda i: (0, 0))],
        out_specs=tok,
        out_shape=jax.ShapeDtypeStruct((nt, 1, tm), jnp.int32),
        compiler_params=_cparams("parallel"),
        name="dest",
    )(bucket, rank, offsets)


def _scatter_kernel(dest_ref, h_ref, init_ref, xs_ref, sem):
    del init_ref
    tm = h_ref.shape[0]
    base = pl.program_id(0) * tm

    for r in range(tm):
        pltpu.make_async_copy(h_ref.at[pl.ds(r, 1)], xs_ref.at[pl.ds(dest_ref[base + r], 1)], sem).start()
    pltpu.make_async_copy(h_ref, xs_ref.at[pl.ds(0, tm)], sem).wait()


def _scatter_rows(dest, hext, n_rows):
    T, W = hext.shape
    tm = DISPATCH_TILE
    init = jnp.zeros((n_rows, W), hext.dtype)
    return pl.pallas_call(
        _scatter_kernel,
        grid_spec=pltpu.PrefetchScalarGridSpec(
            num_scalar_prefetch=1,
            grid=(T // tm,),
            in_specs=[pl.BlockSpec((tm, W), lambda i, d: (i, 0)), pl.BlockSpec(memory_space=pl.ANY)],
            out_specs=pl.BlockSpec(memory_space=pl.ANY),
            scratch_shapes=[pltpu.SemaphoreType.DMA],
        ),
        out_shape=jax.ShapeDtypeStruct((n_rows, W), hext.dtype),
        input_output_aliases={2: 0},
        compiler_params=_cparams("arbitrary"),
        name="scatter",
    )(dest, hext, init)


def _moe_kernel(elo_ref, ehi_ref, nt_ref, xs_ref, wgu_lo_ref, wd_lo_ref, wgu_hi_ref, wd_hi_ref, ys_ref):
    del elo_ref, ehi_ref
    used = pl.program_id(0) < nt_ref[0]

    @pl.when(jnp.logical_not(used))
    def _():
        ys_ref[...] = jnp.zeros_like(ys_ref)

    @pl.when(used)
    def _():
        h = xs_ref[:, :D_MODEL].astype(BF16)
        gates = xs_ref[:, D_MODEL:]
        gu_lo = jnp.dot(h, wgu_lo_ref[0], preferred_element_type=F32)
        gu_hi = jnp.dot(h, wgu_hi_ref[0], preferred_element_type=F32)

        def down(gu, wd_ref):
            g, u = gu[:, :D_EXPERT], gu[:, D_EXPERT:]
            hid = (g * jax.nn.sigmoid(g)) * u
            return jnp.dot(hid.astype(BF16), wd_ref[0], preferred_element_type=F32)

        ys_ref[...] = down(gu_lo, wd_lo_ref) * gates[:, 0:1] + down(gu_hi, wd_hi_ref) * gates[:, 1:2]


def _moe(tile_elo, tile_ehi, n_tiles_used, xs, w_gu, w_d):
    n_rows = xs.shape[0]
    tile = MOE_TILE
    rows = lambda w: pl.BlockSpec((tile, w), lambda i, elo, ehi, nt: (jnp.minimum(i, nt[0] - 1), 0))
    wspec = lambda shape, which: pl.BlockSpec(
        (1,) + shape, lambda i, elo, ehi, nt: ((elo, ehi)[which][i], 0, 0))
    return pl.pallas_call(
        _moe_kernel,
        grid_spec=pltpu.PrefetchScalarGridSpec(
            num_scalar_prefetch=3,
            grid=(n_rows // tile,),
            in_specs=[
                rows(EXT_WIDTH),
                wspec((D_MODEL, 2 * D_EXPERT), 0), wspec((D_EXPERT, D_MODEL), 0),
                wspec((D_MODEL, 2 * D_EXPERT), 1), wspec((D_EXPERT, D_MODEL), 1),
            ],
            out_specs=pl.BlockSpec((tile, D_MODEL), lambda i, elo, ehi, nt: (i, 0)),
        ),
        out_shape=jax.ShapeDtypeStruct((n_rows, D_MODEL), F32),
        compiler_params=_cparams("arbitrary"),
        name="moe",
    )(tile_elo, tile_ehi, n_tiles_used, xs, w_gu, w_d, w_gu, w_d)


def _final_kernel(dest_ref, ys_ref, x1_ref, fn_ref, o_ref, buf_ref, sem):
    tm = x1_ref.shape[0]
    i = pl.program_id(0)
    slot = i % 2

    def start_gather(t, s):
        for r in range(tm):
            pltpu.make_async_copy(ys_ref.at[pl.ds(dest_ref[t * tm + r], 1)], buf_ref.at[s, pl.ds(r, 1)],
                                  sem.at[s]).start()

    @pl.when(i == 0)
    def _():
        start_gather(0, 0)

    @pl.when(i + 1 < pl.num_programs(0))
    def _():
        start_gather(i + 1, 1 - slot)

    pltpu.make_async_copy(ys_ref.at[pl.ds(0, tm)], buf_ref.at[slot], sem.at[slot]).wait()
    o_ref[...] = _rms(x1_ref[...] + buf_ref[slot]) * fn_ref[...]


def _final(dest, ys, x1, fn):
    T, D = x1.shape
    tm = ROW_TILE
    return pl.pallas_call(
        _final_kernel,
        grid_spec=pltpu.PrefetchScalarGridSpec(
            num_scalar_prefetch=1,
            grid=(T // tm,),
            in_specs=[
                pl.BlockSpec(memory_space=pl.ANY),
                pl.BlockSpec((tm, D), lambda i, d: (i, 0)),
                pl.BlockSpec((1, D), lambda i, d: (0, 0)),
            ],
            out_specs=pl.BlockSpec((tm, D), lambda i, d: (i, 0)),
            scratch_shapes=[pltpu.VMEM((2, tm, D), F32), pltpu.SemaphoreType.DMA((2,))],
        ),
        out_shape=jax.ShapeDtypeStruct((T, D), F32),
        compiler_params=_cparams("arbitrary"),
        name="final",
    )(dest, ys, x1, fn)


def _pair_tables():
    lo, hi = [], []
    for g in range(N_GROUPS):
        for a in range(EXPERTS_PER_GROUP):
            for b in range(a + 1, EXPERTS_PER_GROUP):
                lo.append(g * EXPERTS_PER_GROUP + a)
                hi.append(g * EXPERTS_PER_GROUP + b)
    return np.asarray(lo, np.int32), np.asarray(hi, np.int32)


def _attention_weights(w_in, q_norm, w_uq, kv_norm, w_ukv):
    D = w_in.shape[0]
    c0 = 3 * SB_WIDTH
    zeros = lambda r, c: jnp.zeros((r, c), F32)
    w_kr = jnp.concatenate(
        [zeros(D, MLA_NOPE), w_in[:, c0 + Q_LORA + KV_LORA:], zeros(D, LANES - MLA_NOPE - MLA_ROPE)], axis=1)
    dq = MLA_NOPE + MLA_ROPE
    uq = jnp.concatenate(
        [jnp.concatenate([w_uq[:, h * dq:(h + 1) * dq], zeros(Q_LORA, LANES - dq)], axis=1)
         for h in range(MLA_HEADS)], axis=1)
    dkv = MLA_NOPE + MLA_V
    uk = jnp.concatenate(
        [jnp.concatenate([w_ukv[:, h * dkv:h * dkv + MLA_NOPE], zeros(KV_LORA, LANES - MLA_NOPE)], axis=1)
         for h in range(MLA_HEADS)], axis=1)
    uv = jnp.concatenate(
        [jnp.concatenate([w_ukv[:, h * dkv + MLA_NOPE:(h + 1) * dkv].T, zeros(LANES - MLA_V, KV_LORA)], axis=0)
         for h in range(MLA_HEADS)], axis=0)
    return {
        "sb": w_in[:, :2 * SB_WIDTH].astype(BF16),
        "vsb": w_in[:, 2 * SB_WIDTH:c0].T.astype(BF16),
        "cq": w_in[:, c0:c0 + Q_LORA].astype(BF16),
        "ckv": w_in[:, c0 + Q_LORA:c0 + Q_LORA + KV_LORA].astype(BF16),
        "kr": w_kr.astype(BF16),
        "qn": q_norm[None, :],
        "uq": uq.astype(BF16),
        "kvn": kv_norm[None, :],
        "uk": uk.astype(BF16),
        "uv": uv.astype(BF16),
    }


def _router_weights(w_group, b_group, w_expert, b_expert):
    D = w_group.shape[0]
    pad_g = SUBLANES - N_GROUPS
    pad_e = LANES - SUBLANES - N_EXPERTS
    w = jnp.concatenate([w_group, jnp.zeros((D, pad_g), F32), w_expert, jnp.zeros((D, pad_e), F32)], axis=1)
    b = jnp.concatenate([b_group, jnp.zeros((pad_g,), F32), b_expert, jnp.zeros((pad_e,), F32)])[None, :]
    return jnp.concatenate(_split_bf16(w), axis=1), b


def _bucket_layout(counts, n_tiles):
    c = counts[:N_BUCKETS, 0].astype(jnp.int32)
    tiles = (c + MOE_TILE - 1) // MOE_TILE
    tile_end = jnp.cumsum(tiles)
    starts = (tile_end - tiles) * MOE_TILE
    offsets = jnp.concatenate([starts, jnp.zeros((LANES - N_BUCKETS,), jnp.int32)])[:, None]
    n_used = tile_end[-1]
    tile_id = jnp.minimum(jnp.arange(n_tiles, dtype=jnp.int32), n_used - 1)
    tile_bucket = jnp.sum(tile_id[:, None] >= tile_end[None, :], axis=1)
    onehot = tile_bucket[:, None] == jnp.arange(N_BUCKETS)[None, :]
    pick = lambda table: jnp.sum(jnp.where(onehot, table[None, :], 0), axis=1).astype(jnp.int32)
    pair_lo, pair_hi = _pair_tables()
    return offsets, pick(jnp.asarray(pair_lo)), pick(jnp.asarray(pair_hi)), n_used[None]


def kernel(x, positions, attn_norm, w_in, q_norm, w_uq, kv_norm, w_ukv, sb_out_norm, mla_out_norm, w_out,
           ffn_norm, w_group_router, b_group_router, w_expert_router, b_expert_router, w_gate, w_up, w_down,
           final_norm):
    B, S, D = x.shape
    T = B * S
    depth = w_in.shape[0]
    assert D == D_MODEL and T % ROW_TILE == 0 and T % DISPATCH_TILE == 0 and S % ATT_TILE == 0
    assert depth == 1, "the final norm is fused into the last layer's gather kernel"
    n_sorted_tiles = T // MOE_TILE + N_BUCKETS
    n_sorted_rows = n_sorted_tiles * MOE_TILE

    lane = jnp.arange(LANES)
    invf = (ROPE_BASE ** (-(lane % ROPE_HALF).astype(F32) / ROPE_HALF))[None, :]
    pos2d = positions.reshape(T, 1)
    x2d = x.reshape(T, D)
    for l in range(depth):
        aw = _attention_weights(w_in[l], q_norm[l], w_uq[l], kv_norm[l], w_ukv[l])
        q_sb, k_sb, v_sb, q_m, k_m, v_m = _projections(x2d, pos2d, invf, attn_norm[l][None, :], aw)
        o_sb = _sb_attention(q_sb, k_sb, v_sb, B, S)
        o_mla = _mla_attention(q_m, k_m, v_m, B, S)

        wr, br = _router_weights(w_group_router[l], b_group_router[l], w_expert_router[l], b_expert_router[l])
        x1, hext, bucket, rank, counts = _route(
            o_sb, o_mla, x2d, sb_out_norm[l][None, :], mla_out_norm[l][None, :], w_out[l].astype(BF16),
            ffn_norm[l][None, :], wr, br)
        offsets, tile_elo, tile_ehi, n_used = _bucket_layout(counts, n_sorted_tiles)
        dest = _dest_rows(bucket, rank, offsets).reshape(T)

        xs = _scatter_rows(dest, hext, n_sorted_rows)
        w_gu = jnp.concatenate([w_gate[l], w_up[l]], axis=-1).astype(BF16)
        ys = _moe(tile_elo, tile_ehi, n_used, xs, w_gu, w_down[l].astype(BF16))
        x2d = _final(dest, ys, x1, final_norm[None, :])
    return x2d.reshape(B, S, D)
```
